```python
import math
import jax
import jax.numpy as jnp
from jax import lax
import numpy as np

D_MODEL = 1024
BATCH = 4
SEQ = 8192
DEPTH = 4

GRID_W = 64
CTX_LEN = 256
N_SUB = 3
N_MOD = 3 * N_SUB
D_FF = 2816
HALF_STEP = 0.5
ALPHA = (2 * DEPTH) ** 0.25
BETA = (8 * DEPTH) ** -0.25
LN_EPS = 1e-5
RG_WIDTH = D_MODEL
RG_BLOCKS = 16
RG_BLOCK = RG_WIDTH // RG_BLOCKS
RG_C = 8.0
SHORT_CONV = 4
SHORT_PAD = (1, 2)
ML_WIDTH = D_MODEL
ML_HEADS = 4
ML_HEAD_DIM = ML_WIDTH // ML_HEADS
ML_QKV_BLOCK = 4
ML_CHUNK = 64
EVEN_IN = 2 * RG_WIDTH + 2 * ML_WIDTH
EVEN_MIX = RG_WIDTH + ML_WIDTH
CV_WIDTH = D_MODEL // 2
CV_KERNEL = 31
CV_PAD = (15, 15)
DA_HEADS = 8
DA_HEAD_DIM = 64
DA_V_DIM = 2 * DA_HEAD_DIM
DA_WIDTH = DA_HEADS * DA_V_DIM
ODD_IN = 2 * CV_WIDTH + 3 * DA_WIDTH
ODD_MIX = CV_WIDTH + DA_WIDTH
DA_Q_BLOCK = 128
ROPE_BASE = 10000.0
N_EVEN = (DEPTH + 1) // 2
N_ODD = DEPTH // 2

kernel_name = 'hybrid_rglru_mlstm_conformer_diffattn_dit'


def _standardize(x, eps=LN_EPS):
    xf = x.astype(jnp.float32)
    mu = jnp.mean(xf, axis=-1, keepdims=True)
    var = jnp.mean(jnp.square(xf - mu), axis=-1, keepdims=True)
    return (xf - mu) * lax.rsqrt(var + eps)


def layer_norm(x, g, b):
    return (_standardize(x) * g.astype(jnp.float32) + b.astype(jnp.float32)).astype(x.dtype)


def rms_norm(x, g, eps=LN_EPS):
    xf = x.astype(jnp.float32)
    y = xf * lax.rsqrt(jnp.mean(jnp.square(xf), axis=-1, keepdims=True) + eps)
    return (y * g.astype(jnp.float32)).astype(x.dtype)


def depthwise_conv(x, w, b, pad):
    y = lax.conv_general_dilated(x, w[:, None, :].astype(x.dtype), window_strides=(1,), padding=[pad],
                                 dimension_numbers=('NWC', 'WIO', 'NWC'), feature_group_count=x.shape[-1])
    return y + b.astype(x.dtype)


def block_diag(x, w):
    g, di, do = w.shape
    xg = x.reshape(x.shape[:-1] + (g, di))
    return jnp.einsum('btgi,gio->btgo', xg, w).reshape(x.shape[:-1] + (g * do,))


def swiglu(h, wg, wu, wd):
    return (jax.nn.silu(h @ wg) * (h @ wu)) @ wd


def modulate(h, mod, s):
    return h * (1.0 + mod[:, :, 3 * s + 1]) + mod[:, :, 3 * s]


def residual_post_norm(x, y, mod, s, ln_g, ln_b):
    return layer_norm(ALPHA * x + mod[:, :, 3 * s + 2] * y, ln_g[s], ln_b[s])


def ffn_sublayer(x, mod, s, wg, wu, wd, ln_g, ln_b):
    y = HALF_STEP * swiglu(modulate(x, mod, s), wg, wu, wd)
    return residual_post_norm(x, y, mod, s, ln_g, ln_b)


def _maybe_flip(a, rev):
    return jnp.flip(a, axis=1) if rev else a


def _linear_combine(left, right):
    a1, b1 = left
    a2, b2 = right
    return a1 * a2, a2 * b1 + b2


def rglru_direction(u, w_a, b_a, w_x, b_x, lam, h0, reverse):
    r = jax.nn.sigmoid((block_diag(u, w_a) + b_a).astype(jnp.float32))
    i = jax.nn.sigmoid((block_diag(u, w_x) + b_x).astype(jnp.float32))
    log_a = -RG_C * r * jax.nn.softplus(-lam.astype(jnp.float32))
    b = jnp.sqrt(-jnp.expm1(2.0 * log_a)) * i * u.astype(jnp.float32)
    a_cum, b_cum = lax.associative_scan(_linear_combine, (jnp.exp(log_a), b), reverse=reverse, axis=1)
    return a_cum * h0[:, None, :] + b_cum


def mlstm_chunkwise(q, k, v, i_pre, log_f, state):
    bsz, t_len, n_h, dh = q.shape
    n_chunks = t_len // ML_CHUNK

    def to_chunks(a):
        return jnp.moveaxis(a.reshape((bsz, n_chunks, ML_CHUNK) + a.shape[2:]), 1, 0)

    xs = tuple(to_chunks(a) for a in (q, k, v, i_pre, log_f))
    causal = jnp.tril(jnp.ones((ML_CHUNK, ML_CHUNK), dtype=bool))

    def step(carry, inp):
        c_st, n_st, m_st = carry
        qc, kc, vc, ic, fc = inp
        b = jnp.cumsum(fc, axis=1).transpose(0, 2, 1)
        ig = ic.transpose(0, 2, 1)
        g = b[..., -1]
        log_w = jnp.where(causal, b[..., :, None] - b[..., None, :] + ig[..., None, :], -jnp.inf)
        log_inter = b + m_st[..., None]
        m_t = jnp.maximum(log_inter, jnp.max(log_w, axis=-1))
        w = jnp.exp(log_w - m_t[..., None]) * jnp.einsum('blhd,bshd->bhls', qc, kc)
        s_inter = jnp.exp(log_inter - m_t)
        num = jnp.einsum('bhls,bshd->blhd', w, vc) + jnp.einsum('bhl,bhed,blhd->blhe', s_inter, c_st, qc)
        den = jnp.sum(w, axis=-1) + s_inter * jnp.einsum('bhd,blhd->bhl', n_st, qc)
        h = num / jnp.maximum(jnp.abs(den), jnp.exp(-m_t)).transpose(0, 2, 1)[..., None]
        log_end = g[..., None] - b + ig
        m_new = jnp.maximum(g + m_st, jnp.max(log_end, axis=-1))
        e = jnp.exp(log_end - m_new[..., None])
        decay = jnp.exp(g + m_st - m_new)
        c_new = decay[..., None, None] * c_st + jnp.einsum('bhs,bshe,bshd->bhed', e, vc, kc)
        n_new = decay[..., None] * n_st + jnp.einsum('bhs,bshd->bhd', e, kc)
        return (c_new, n_new, m_new), h

    state, hs = lax.scan(step, state, xs)
    return jnp.moveaxis(hs, 0, 1).reshape(bsz, t_len, n_h, dh), state


def even_mixer(h_lat, h_ctx, need_ctx, w_in, w_out, rg_conv_w, rg_conv_b, rg_w_a, rg_b_a, rg_w_x, rg_b_x,
               rg_lambda, ml_conv_w, ml_conv_b, ml_w_q, ml_w_k, ml_w_v, ml_w_gate, ml_b_gate, ml_norm_g, ml_skip):
    cuts = [RG_WIDTH, 2 * RG_WIDTH, 2 * RG_WIDTH + ML_WIDTH]

    def to_heads(a):
        return a.reshape(a.shape[:2] + (ML_HEADS, ML_HEAD_DIM)).astype(jnp.float32)

    def prep(h):
        rg_x, rg_gate, ml_x, ml_z = jnp.split(h @ w_in, cuts, axis=-1)
        u = depthwise_conv(rg_x, rg_conv_w, rg_conv_b, SHORT_PAD)
        ml_c = jax.nn.silu(depthwise_conv(ml_x, ml_conv_w, ml_conv_b, SHORT_PAD))
        q, k, v = block_diag(ml_c, ml_w_q), block_diag(ml_c, ml_w_k), block_diag(ml_x, ml_w_v)
        qkv = jnp.concatenate([q, k, v], axis=-1)
        heads = (to_heads(q), to_heads(k) * ML_HEAD_DIM ** -0.5, to_heads(v))
        return rg_gate, ml_z, u, ml_c, qkv, heads

    def gates(qkv, d):
        pre = (qkv @ ml_w_gate[d] + ml_b_gate[d]).astype(jnp.float32)
        return pre[..., :ML_HEADS], jax.nn.log_sigmoid(pre[..., ML_HEADS:])

    rgg_l, mlz_l, u_l, mlc_l, qkv_l, heads_l = prep(h_lat)
    rgg_c, mlz_c, u_c, mlc_c, qkv_c, heads_c = prep(h_ctx)
    bsz = h_ctx.shape[0]
    rg_out_l, rg_out_c, ml_out_l, ml_out_c = [], [], [], []
    for d, rev in enumerate((False, True)):
        h0 = jnp.zeros((bsz, RG_WIDTH), jnp.float32)
        hc = rglru_direction(u_c, rg_w_a[d], rg_b_a[d], rg_w_x[d], rg_b_x[d], rg_lambda[d], h0, rev)
        h_end = hc[:, 0] if rev else hc[:, -1]
        hl = rglru_direction(u_l, rg_w_a[d], rg_b_a[d], rg_w_x[d], rg_b_x[d], rg_lambda[d], h_end, rev)
        rg_out_c.append(hc)
        rg_out_l.append(hl)
        state0 = (jnp.zeros((bsz, ML_HEADS, ML_HEAD_DIM, ML_HEAD_DIM), jnp.float32),
                  jnp.zeros((bsz, ML_HEADS, ML_HEAD_DIM), jnp.float32),
                  jnp.zeros((bsz, ML_HEADS), jnp.float32))
        ins_c = [_maybe_flip(a, rev) for a in heads_c + gates(qkv_c, d)]
        mc, st = mlstm_chunkwise(*ins_c, state0)
        ins_l = [_maybe_flip(a, rev) for a in heads_l + gates(qkv_l, d)]
        ml, _ = mlstm_chunkwise(*ins_l, st)
        ml_out_c.append(_maybe_flip(mc, rev))
        ml_out_l.append(_maybe_flip(ml, rev))

    def finish(rg_gate, ml_z, ml_c, h_rg, h_ml):
        y_rg = h_rg.astype(rg_gate.dtype) * jax.nn.gelu(rg_gate)
        hn = _standardize(h_ml).reshape(h_ml.shape[:2] + (ML_WIDTH,)) * ml_norm_g.astype(jnp.float32)
        y_ml = (hn.astype(ml_c.dtype) + ml_skip * ml_c) * jax.nn.silu(ml_z)
        return jnp.concatenate([y_rg, y_ml], axis=-1) @ w_out

    y_lat = finish(rgg_l, mlz_l, mlc_l, rg_out_l[0] + rg_out_l[1], ml_out_l[0] + ml_out_l[1])
    y_ctx = finish(rgg_c, mlz_c, mlc_c, rg_out_c[0] + rg_out_c[1], ml_out_c[0] + ml_out_c[1]) if need_ctx else None
    return y_lat, y_ctx


def axial_rope_tables(n_tokens):
    rows = n_tokens // GRID_W
    row = jnp.repeat(jnp.arange(rows), GRID_W).astype(jnp.float32)
    col = jnp.tile(jnp.arange(GRID_W), rows).astype(jnp.float32)
    axis_dim = DA_HEAD_DIM // 2
    inv_freq = ROPE_BASE ** (-jnp.arange(0, axis_dim, 2, dtype=jnp.float32) / axis_dim)
    ang_r = row[:, None] * inv_freq
    ang_c = col[:, None] * inv_freq
    return jnp.cos(ang_r), jnp.sin(ang_r), jnp.cos(ang_c), jnp.sin(ang_c)


def _rope_half(x, cos, sin):
    f = cos.shape[-1]
    c = cos[None, :, None, None, :].astype(x.dtype)
    s = sin[None, :, None, None, :].astype(x.dtype)
    x1, x2 = x[..., :f], x[..., f:]
    return jnp.concatenate([x1 * c - x2 * s, x2 * c + x1 * s], axis=-1)


def apply_axial_rope(x, tables):
    cos_r, sin_r, cos_c, sin_c = tables
    half = x.shape[-1] // 2
    return jnp.concatenate([_rope_half(x[..., :half], cos_r, sin_r),
                            _rope_half(x[..., half:], cos_c, sin_c)], axis=-1)


def diff_attention(q, k, v, lam):
    s = jnp.einsum('bqhmd,bkhmd->bhmqk', q, k, preferred_element_type=jnp.float32) * DA_HEAD_DIM ** -0.5
    p = jax.nn.softmax(s, axis=-1)
    w = p[:, :, 0] - lam * p[:, :, 1]
    return jnp.einsum('bhqk,bkhe->bqhe', w.astype(v.dtype), v)


def conformer_conv(glu_in, conv_w, conv_b, ln_g, ln_b):
    a, gate = jnp.split(glu_in, 2, axis=-1)
    y = depthwise_conv(a * jax.nn.sigmoid(gate), conv_w, conv_b, CV_PAD)
    return jax.nn.silu(layer_norm(y, ln_g, ln_b))


def odd_mixer(h_lat, h_ctx, need_ctx, layer, rope, w_in, w_out, cv_conv_w, cv_conv_b, cv_ln_g, cv_ln_b,
              da_lambda, da_subln_g):
    cuts = [2 * CV_WIDTH, 2 * CV_WIDTH + DA_WIDTH, 2 * CV_WIDTH + 2 * DA_WIDTH]
    glu_l, q_l, k_l, v_l = jnp.split(h_lat @ w_in, cuts, axis=-1)
    if need_ctx:
        glu_c, q_c, k_c, v_c = jnp.split(h_ctx @ w_in, cuts, axis=-1)
    else:
        k_c, v_c = jnp.split(h_ctx @ w_in[:, cuts[1]:], 2, axis=-1)
    lam_init = 0.8 - 0.6 * math.exp(-0.3 * layer)
    lamf = da_lambda.astype(jnp.float32)
    lam = jnp.exp(jnp.sum(lamf[0] * lamf[1])) - jnp.exp(jnp.sum(lamf[2] * lamf[3])) + lam_init

    def qk_heads(a):
        return a.reshape(a.shape[:2] + (DA_HEADS, 2, DA_HEAD_DIM))

    def v_heads(a):
        return a.reshape(a.shape[:2] + (DA_HEADS, DA_V_DIM))

    def finish_attn(a):
        out = rms_norm(a, da_subln_g) * (1.0 - lam_init)
        return out.reshape(a.shape[:2] + (DA_WIDTH,))

    kh_c, vh_c = qk_heads(k_c), v_heads(v_c)
    qh_l = apply_axial_rope(qk_heads(q_l), rope)
    kh_l = apply_axial_rope(qk_heads(k_l), rope)
    k_all = jnp.concatenate([kh_c, kh_l], axis=1)
    v_all = jnp.concatenate([vh_c, v_heads(v_l)], axis=1)
    bsz, t_len = h_lat.shape[:2]
    n_blk = t_len // DA_Q_BLOCK
    q_blocks = jnp.moveaxis(qh_l.reshape((bsz, n_blk, DA_Q_BLOCK) + qh_l.shape[2:]), 1, 0)
    attn = lax.map(lambda qb: diff_attention(qb, k_all, v_all, lam), q_blocks)
    attn_l = jnp.moveaxis(attn, 0, 1).reshape((bsz, t_len, DA_HEADS, DA_V_DIM))
    conv_l = conformer_conv(glu_l, cv_conv_w, cv_conv_b, cv_ln_g, cv_ln_b)
    y_lat = jnp.concatenate([conv_l, finish_attn(attn_l)], axis=-1) @ w_out
    y_ctx = None
    if need_ctx:
        attn_c = diff_attention(qk_heads(q_c), kh_c, vh_c, lam)
        conv_c = conformer_conv(glu_c, cv_conv_w, cv_conv_b, cv_ln_g, cv_ln_b)
        y_ctx = jnp.concatenate([conv_c, finish_attn(attn_c)], axis=-1) @ w_out
    return y_lat, y_ctx


def setup_inputs(seed: int = 0) -> dict:
    key = jax.random.key(seed)
    keys = iter(jax.random.split(key, 48))

    def normal(shape, scale):
        return jax.random.normal(next(keys), shape, jnp.float32) * scale

    d, f, n_l, n_e, n_o, h = D_MODEL, D_FF, DEPTH, N_EVEN, N_ODD, ML_HEADS
    gate_rows = jnp.tile(jnp.array([0.0, 0.0, 1.0], jnp.float32), N_SUB)[:, None] * jnp.ones((1, d), jnp.float32)
    lru_a = jax.random.uniform(next(keys), (n_e, 2, RG_WIDTH), jnp.float32, 0.9, 0.999) ** (1.0 / RG_C)
    n_qkv_blocks = ML_WIDTH // ML_QKV_BLOCK
    return {
        'x': normal((BATCH, SEQ, d), 1.0),
        'c': normal((BATCH, d), 1.0),
        'ctx': normal((BATCH, CTX_LEN, d), 1.0),
        'c_ctx': normal((d,), 1.0),
        'w_ada': normal((n_l, d, N_MOD * d), 0.2 * d ** -0.5),
        'b_ada': (gate_rows + normal((n_l, N_MOD, d), 0.02)).reshape(n_l, N_MOD * d),
        'ln_g': 1.0 + normal((n_l, N_SUB, d), 0.02),
        'ln_b': normal((n_l, N_SUB, d), 0.02),
        'ffn_w_gate': normal((n_l, 2, d, f), d ** -0.5),
        'ffn_w_up': normal((n_l, 2, d, f), d ** -0.5),
        'ffn_w_down': normal((n_l, 2, f, d), BETA * f ** -0.5),
        'ev_w_in': normal((n_e, d, EVEN_IN), d ** -0.5),
        'ev_w_out': normal((n_e, EVEN_MIX, d), BETA * EVEN_MIX ** -0.5),
        'rg_conv_w': normal((n_e, SHORT_CONV, RG_WIDTH), SHORT_CONV ** -0.5),
        'rg_conv_b': normal((n_e, RG_WIDTH), 0.02),
        'rg_w_a': normal((n_e, 2, RG_BLOCKS, RG_BLOCK, RG_BLOCK), RG_BLOCK ** -0.5),
        'rg_b_a': normal((n_e, 2, RG_WIDTH), 0.02),
        'rg_w_x': normal((n_e, 2, RG_BLOCKS, RG_BLOCK, RG_BLOCK), RG_BLOCK ** -0.5),
        'rg_b_x': normal((n_e, 2, RG_WIDTH), 0.02),
        'rg_lambda': jnp.log(lru_a) - jnp.log1p(-lru_a),
        'ml_conv_w': normal((n_e, SHORT_CONV, ML_WIDTH), SHORT_CONV ** -0.5),
        'ml_conv_b': normal((n_e, ML_WIDTH), 0.02),
        'ml_w_q': normal((n_e, n_qkv_blocks, ML_QKV_BLOCK, ML_QKV_BLOCK), ML_QKV_BLOCK ** -0.5),
        'ml_w_k': normal((n_e, n_qkv_blocks, ML_QKV_BLOCK, ML_QKV_BLOCK), ML_QKV_BLOCK ** -0.5),
        'ml_w_v': normal((n_e, n_qkv_blocks, ML_QKV_BLOCK, ML_QKV_BLOCK), ML_QKV_BLOCK ** -0.5),
        'ml_w_gate': normal((n_e, 2, 3 * ML_WIDTH, 2 * h), (3 * ML_WIDTH) ** -0.5),
        'ml_b_gate': jnp.concatenate([normal((n_e, 2, h), 0.1),
                                      jnp.linspace(3.0, 6.0, h, dtype=jnp.float32) + normal((n_e, 2, h), 0.1)], axis=-1),
        'ml_norm_g': 1.0 + normal((n_e, ML_WIDTH), 0.02),
        'ml_skip': 1.0 + normal((n_e, ML_WIDTH), 0.02),
        'od_w_in': normal((n_o, d, ODD_IN), d ** -0.5),
        'od_w_out': normal((n_o, ODD_MIX, d), BETA * ODD_MIX ** -0.5),
        'cv_conv_w': normal((n_o, CV_KERNEL, CV_WIDTH), CV_KERNEL ** -0.5),
        'cv_conv_b': normal((n_o, CV_WIDTH), 0.02),
        'cv_ln_g': 1.0 + normal((n_o, CV_WIDTH), 0.02),
        'cv_ln_b': normal((n_o, CV_WIDTH), 0.02),
        'da_lambda': normal((n_o, 4, DA_HEAD_DIM), 0.1),
        'da_subln_g': 1.0 + normal((n_o, DA_V_DIM), 0.02),
    }


def reference(x, c, ctx, c_ctx, w_ada, b_ada, ln_g, ln_b, ffn_w_gate, ffn_w_up, ffn_w_down,
              ev_w_in, ev_w_out, rg_conv_w, rg_conv_b, rg_w_a, rg_b_a, rg_w_x, rg_b_x, rg_lambda,
              ml_conv_w, ml_conv_b, ml_w_q, ml_w_k, ml_w_v, ml_w_gate, ml_b_gate, ml_norm_g, ml_skip,
              od_w_in, od_w_out, cv_conv_w, cv_conv_b, cv_ln_g, cv_ln_b, da_lambda, da_subln_g):
    rope = axial_rope_tables(x.shape[1])
    x_lat, x_ctx = x, ctx
    cond_lat = jax.nn.silu(c)
    cond_ctx = jax.nn.silu(c_ctx)
    bsz = x.shape[0]
    for layer in range(DEPTH):
        need_ctx = layer < DEPTH - 1
        mod_l = (cond_lat @ w_ada[layer] + b_ada[layer]).reshape(bsz, 1, N_MOD, D_MODEL)
        mod_c = (cond_ctx @ w_ada[layer] + b_ada[layer]).reshape(1, 1, N_MOD, D_MODEL)
        g, b = ln_g[layer], ln_b[layer]
        ffn1 = (ffn_w_gate[layer, 0], ffn_w_up[layer, 0], ffn_w_down[layer, 0])
        ffn2 = (ffn_w_gate[layer, 1], ffn_w_up[layer, 1], ffn_w_down[layer, 1])
        x_lat = ffn_sublayer(x_lat, mod_l, 0, *ffn1, g, b)
        x_ctx = ffn_sublayer(x_ctx, mod_c, 0, *ffn1, g, b)
        h_lat = modulate(x_lat, mod_l, 1)
        h_ctx = modulate(x_ctx, mod_c, 1)
        i = layer // 2
        if layer % 2 == 0:
            y_lat, y_ctx = even_mixer(h_lat, h_ctx, need_ctx, ev_w_in[i], ev_w_out[i], rg_conv_w[i], rg_conv_b[i],
                                      rg_w_a[i], rg_b_a[i], rg_w_x[i], rg_b_x[i], rg_lambda[i],
                                      ml_conv_w[i], ml_conv_b[i], ml_w_q[i], ml_w_k[i], ml_w_v[i],
                                      ml_w_gate[i], ml_b_gate[i], ml_norm_g[i], ml_skip[i])
        else:
            y_lat, y_ctx = odd_mixer(h_lat, h_ctx, need_ctx, layer, rope, od_w_in[i], od_w_out[i],
                                     cv_conv_w[i], cv_conv_b[i], cv_ln_g[i], cv_ln_b[i],
                                     da_lambda[i], da_subln_g[i])
        x_lat = residual_post_norm(x_lat, y_lat, mod_l, 1, g, b)
        x_lat = ffn_sublayer(x_lat, mod_l, 2, *ffn2, g, b)
        if need_ctx:
            x_ctx = residual_post_norm(x_ctx, y_ctx, mod_c, 1, g, b)
            x_ctx = ffn_sublayer(x_ctx, mod_c, 2, *ffn2, g, b)
    return x_lat
```

```python
import functools
import math

import jax
import jax.numpy as jnp
from jax import lax
from jax.experimental import pallas as pl
from jax.experimental.pallas import tpu as pltpu

F32 = jnp.float32
BF16 = jnp.bfloat16

GRID_W = 64
N_SUB = 3
N_MOD = 3 * N_SUB
HALF_STEP = 0.5
LN_EPS = 1e-5
RG_C = 8.0
SHORT_PAD = (1, 2)
ML_HEADS = 4
ML_CHUNK = 64
CV_PAD = (15, 15)
DA_HEADS = 8
DA_HEAD_DIM = 64
DA_V_DIM = 2 * DA_HEAD_DIM
ROPE_BASE = 10000.0

FFN_CHUNK = 256
VMEM_LIMIT = 56 * 1024 * 1024


def _params(*sem):
    return pltpu.CompilerParams(dimension_semantics=sem, vmem_limit_bytes=VMEM_LIMIT)


def _row_tile(t, want):
    tm = min(t, want)
    assert t % tm == 0
    return tm


def _layer_norm(z, g, b):
    mu = jnp.mean(z, axis=-1, keepdims=True)
    zc = z - mu
    var = jnp.mean(zc * zc, axis=-1, keepdims=True)
    return zc * lax.rsqrt(var + LN_EPS) * g + b


def _mod_spec(mod):
    per_sample = mod.shape[0] > 1
    return pl.BlockSpec((1,) + mod.shape[1:], lambda b, i: (b if per_sample else 0, 0, 0))


def _const_spec(shape):
    return pl.BlockSpec(shape, lambda b, i: (0,) * len(shape))


def _ffn_body(x_ref, mod_ref, wg_ref, wu_ref, wd_ref, ln_ref, o_ref, *, alpha):
    x = x_ref[0]
    mod = mod_ref[0]
    h = (x * (1.0 + mod[1:2]) + mod[0:1]).astype(BF16)
    y = jnp.zeros(x.shape, F32)
    for c in range(wg_ref.shape[0]):
        g = jnp.dot(h, wg_ref[c], preferred_element_type=F32)
        u = jnp.dot(h, wu_ref[c], preferred_element_type=F32)
        a = (g * jax.nn.sigmoid(g) * u).astype(BF16)
        y = y + jnp.dot(a, wd_ref[c], preferred_element_type=F32)
    z = alpha * x + (HALF_STEP * mod[2:3]) * y
    o_ref[0] = _layer_norm(z, ln_ref[0:1], ln_ref[1:2])


def ffn_sublayer(x, mod, wg, wu, wd, ln, alpha):
    bsz, t, d = x.shape
    tm = _row_tile(t, 512)
    return pl.pallas_call(
        functools.partial(_ffn_body, alpha=alpha),
        grid=(bsz, t // tm),
        in_specs=[pl.BlockSpec((1, tm, d), lambda b, i: (b, i, 0)), _mod_spec(mod),
                  _const_spec(wg.shape), _const_spec(wu.shape), _const_spec(wd.shape), _const_spec(ln.shape)],
        out_specs=pl.BlockSpec((1, tm, d), lambda b, i: (b, i, 0)),
        out_shape=jax.ShapeDtypeStruct(x.shape, F32),
        compiler_params=_params("parallel", "parallel"),
        name="ffn_sublayer",
    )(x, mod, wg, wu, wd, ln)


def _prep_ffn(wg, wu, wd):
    d, f = wg.shape
    n = f // FFN_CHUNK
    to_chunks = lambda w: w.astype(BF16).reshape(d, n, FFN_CHUNK).transpose(1, 0, 2)
    return to_chunks(wg), to_chunks(wu), wd.astype(BF16).reshape(n, FFN_CHUNK, d)


def _inproj_even_body(x_ref, mod_ref, w_ref, *o_refs):
    mod = mod_ref[0]
    h = (x_ref[0] * (1.0 + mod[1:2]) + mod[0:1]).astype(BF16)
    width = o_refs[0].shape[-1]
    for j, o_ref in enumerate(o_refs):
        o_ref[0] = jnp.dot(h, w_ref[:, j * width:(j + 1) * width], preferred_element_type=F32)


def inproj_even(x, mod, w_in):
    bsz, t, d = x.shape
    tm = _row_tile(t, 512)
    n_out = w_in.shape[1] // d
    row_spec = pl.BlockSpec((1, tm, d), lambda b, i: (b, i, 0))
    return pl.pallas_call(
        _inproj_even_body,
        grid=(bsz, t // tm),
        in_specs=[row_spec, _mod_spec(mod), _const_spec(w_in.shape)],
        out_specs=[row_spec] * n_out,
        out_shape=[jax.ShapeDtypeStruct(x.shape, F32)] * n_out,
        compiler_params=_params("parallel", "parallel"),
        name="inproj_even",
    )(x, mod, w_in)


def _rope128(a, cos, sin_lo, sin_hi):
    return a * cos + pltpu.roll(a, 112, 1) * sin_lo + pltpu.roll(a, 16, 1) * sin_hi


def _inproj_odd_body(x_ref, mod_ref, w_ref, *rest, use_rope):
    if use_rope:
        cos_ref, slo_ref, shi_ref, glu_ref, q_ref, k_ref, v_ref = rest
    else:
        glu_ref, q_ref, k_ref, v_ref = rest
    mod = mod_ref[0]
    h = (x_ref[0] * (1.0 + mod[1:2]) + mod[0:1]).astype(BF16)
    cw = glu_ref.shape[-1]
    a = jnp.dot(h, w_ref[:, :cw], preferred_element_type=F32)
    gate = jnp.dot(h, w_ref[:, cw:2 * cw], preferred_element_type=F32)
    glu_ref[0] = a * jax.nn.sigmoid(gate)
    dw = q_ref.shape[-1]
    q = jnp.dot(h, w_ref[:, 2 * cw:2 * cw + dw], preferred_element_type=F32) * (DA_HEAD_DIM ** -0.5)
    k = jnp.dot(h, w_ref[:, 2 * cw + dw:2 * cw + 2 * dw], preferred_element_type=F32)
    if use_rope:
        cos, slo, shi = cos_ref[...], slo_ref[...], shi_ref[...]
        for j in range(dw // 128):
            sl = slice(128 * j, 128 * (j + 1))
            q_ref[0, :, sl] = _rope128(q[:, sl], cos, slo, shi).astype(BF16)
            k_ref[0, :, sl] = _rope128(k[:, sl], cos, slo, shi).astype(BF16)
    else:
        q_ref[0] = q.astype(BF16)
        k_ref[0] = k.astype(BF16)
    v_ref[0] = jnp.dot(h, w_ref[:, 2 * cw + 2 * dw:], preferred_element_type=F32).astype(BF16)


def inproj_odd(x, mod, w_in, rope, cw, dw):
    bsz, t, d = x.shape
    tm = _row_tile(t, 512)
    row = lambda w: pl.BlockSpec((1, tm, w), lambda b, i: (b, i, 0))
    in_specs = [row(d), _mod_spec(mod), _const_spec(w_in.shape)]
    args = [x, mod, w_in]
    if rope is not None:
        in_specs += [pl.BlockSpec((tm, 128), lambda b, i: (i, 0))] * 3
        args += list(rope)
    return pl.pallas_call(
        functools.partial(_inproj_odd_body, use_rope=rope is not None),
        grid=(bsz, t // tm),
        in_specs=in_specs,
        out_specs=[row(cw), row(dw), row(dw), row(dw)],
        out_shape=[jax.ShapeDtypeStruct((bsz, t, cw), F32)] + [jax.ShapeDtypeStruct((bsz, t, dw), BF16)] * 3,
        compiler_params=_params("parallel", "parallel"),
        name="inproj_odd",
    )(*args)


def _rope_tables(t):
    pos = jnp.arange(t)
    row = (pos // GRID_W).astype(F32)
    col = (pos % GRID_W).astype(F32)
    axis_dim = DA_HEAD_DIM // 2
    inv_freq = ROPE_BASE ** (-jnp.arange(0, axis_dim, 2, dtype=F32) / axis_dim)
    ang_r = row[:, None] * inv_freq
    ang_c = col[:, None] * inv_freq
    zeros = jnp.zeros_like(ang_r)
    cos64 = jnp.concatenate([jnp.cos(ang_r)] * 2 + [jnp.cos(ang_c)] * 2, axis=-1)
    sin_lo64 = jnp.concatenate([-jnp.sin(ang_r), zeros, -jnp.sin(ang_c), zeros], axis=-1)
    sin_hi64 = jnp.concatenate([zeros, jnp.sin(ang_r), zeros, jnp.sin(ang_c)], axis=-1)
    return tuple(jnp.tile(a, (1, 2)) for a in (cos64, sin_lo64, sin_hi64))


def _attn_body(lam_ref, q_ref, k_ref, v_ref, g_ref, o_ref, m_sc, l_sc, acc_sc, *, out_scale):
    j = pl.program_id(3)

    @pl.when(j == 0)
    def _():
        m_sc[...] = jnp.full(m_sc.shape, -jnp.inf, F32)
        l_sc[...] = jnp.zeros(l_sc.shape, F32)
        acc_sc[...] = jnp.zeros(acc_sc.shape, F32)

    q, k, v = q_ref[0], k_ref[0], v_ref[0]
    for m in range(2):
        sl = slice(DA_HEAD_DIM * m, DA_HEAD_DIM * (m + 1))
        s = lax.dot_general(q[:, sl], k[:, sl], (((1,), (1,)), ((), ())), preferred_element_type=F32)
        m_prev = m_sc[m]
        m_new = jnp.maximum(m_prev, jnp.max(s, axis=-1, keepdims=True))
        p = jnp.exp(s - m_new)
        corr = jnp.exp(m_prev - m_new)
        l_sc[m] = corr * l_sc[m] + jnp.sum(p, axis=-1, keepdims=True)
        acc_sc[m] = corr * acc_sc[m] + jnp.dot(p.astype(BF16), v, preferred_element_type=F32)
        m_sc[m] = m_new

    @pl.when(j == pl.num_programs(3) - 1)
    def _():
        o = acc_sc[0] / l_sc[0] - lam_ref[0] * (acc_sc[1] / l_sc[1])
        o = o * lax.rsqrt(jnp.mean(o * o, axis=-1, keepdims=True) + LN_EPS)
        o_ref[0] = o * g_ref[...] * out_scale


def diff_attention(q, k, v, lam, subln_g, out_scale, tq, tk):
    bsz, t, dw = q.shape
    s_len = k.shape[1]
    n_heads = dw // DA_V_DIM
    assert t % tq == 0 and s_len % tk == 0
    return pl.pallas_call(
        functools.partial(_attn_body, out_scale=out_scale),
        grid=(bsz, n_heads, t // tq, s_len // tk),
        in_specs=[pl.BlockSpec(memory_space=pltpu.SMEM),
                  pl.BlockSpec((1, tq, DA_V_DIM), lambda b, h, i, j: (b, i, h)),
                  pl.BlockSpec((1, tk, DA_V_DIM), lambda b, h, i, j: (b, j, h)),
                  pl.BlockSpec((1, tk, DA_V_DIM), lambda b, h, i, j: (b, j, h)),
                  pl.BlockSpec((1, DA_V_DIM), lambda b, h, i, j: (0, 0))],
        out_specs=pl.BlockSpec((1, tq, DA_V_DIM), lambda b, h, i, j: (b, i, h)),
        out_shape=jax.ShapeDtypeStruct((bsz, t, dw), F32),
        scratch_shapes=[pltpu.VMEM((2, tq, 1), F32), pltpu.VMEM((2, tq, 1), F32),
                        pltpu.VMEM((2, tq, DA_V_DIM), F32)],
        compiler_params=_params("parallel", "parallel", "parallel", "arbitrary"),
        name="diff_attention",
    )(lam, q, k, v, subln_g)


def _mixout_body(*refs, n_parts, alpha):
    x_ref, mod_ref, ln_ref = refs[:3]
    part_refs, w_refs, o_ref = refs[3:3 + n_parts], refs[3 + n_parts:3 + 2 * n_parts], refs[-1]
    x = x_ref[0]
    y = jnp.zeros(x.shape, F32)
    for p_ref, w_ref in zip(part_refs, w_refs):
        y = y + jnp.dot(p_ref[0].astype(BF16), w_ref[...], preferred_element_type=F32)
    z = alpha * x + mod_ref[0][2:3] * y
    o_ref[0] = _layer_norm(z, ln_ref[0:1], ln_ref[1:2])


def mixer_out(x, mod, ln, parts, weights, alpha):
    bsz, t, d = x.shape
    tm = _row_tile(t, 512)
    row = lambda w: pl.BlockSpec((1, tm, w), lambda b, i: (b, i, 0))
    return pl.pallas_call(
        functools.partial(_mixout_body, n_parts=len(parts), alpha=alpha),
        grid=(bsz, t // tm),
        in_specs=[row(d), _mod_spec(mod), _const_spec(ln.shape)] + [row(p.shape[-1]) for p in parts]
                 + [_const_spec(w.shape) for w in weights],
        out_specs=row(d),
        out_shape=jax.ShapeDtypeStruct(x.shape, F32),
        compiler_params=_params("parallel", "parallel"),
        name="mixer_out",
    )(x, mod, ln, *parts, *weights)


def _depthwise_conv(x, w, b, pad):
    y = lax.conv_general_dilated(x, w[:, None, :].astype(x.dtype), window_strides=(1,), padding=[pad],
                                 dimension_numbers=('NWC', 'WIO', 'NWC'), feature_group_count=x.shape[-1])
    return y + b.astype(x.dtype)


def _block_diag(x, w):
    g, di, do = w.shape
    xg = x.reshape(x.shape[:-1] + (g, di))
    return jnp.einsum('btgi,gio->btgo', xg, w).reshape(x.shape[:-1] + (g * do,))


def _maybe_flip(a, rev):
    return jnp.flip(a, axis=1) if rev else a


def _linear_combine(left, right):
    a1, b1 = left
    a2, b2 = right
    return a1 * a2, a2 * b1 + b2


def _rglru_direction(u, w_a, b_a, w_x, b_x, lam, h0, reverse):
    r = jax.nn.sigmoid((_block_diag(u, w_a) + b_a).astype(F32))
    i = jax.nn.sigmoid((_block_diag(u, w_x) + b_x).astype(F32))
    log_a = -RG_C * r * jax.nn.softplus(-lam.astype(F32))
    b = jnp.sqrt(-jnp.expm1(2.0 * log_a)) * i * u.astype(F32)
    a_cum, b_cum = lax.associative_scan(_linear_combine, (jnp.exp(log_a), b), reverse=reverse, axis=1)
    return a_cum * h0[:, None, :] + b_cum


def _mlstm_chunkwise(q, k, v, i_pre, log_f, state):
    bsz, t_len, n_h, dh = q.shape
    n_chunks = t_len // ML_CHUNK

    def to_chunks(a):
        return jnp.moveaxis(a.reshape((bsz, n_chunks, ML_CHUNK) + a.shape[2:]), 1, 0)

    xs = tuple(to_chunks(a) for a in (q, k, v, i_pre, log_f))
    causal = jnp.tril(jnp.ones((ML_CHUNK, ML_CHUNK), dtype=bool))

    def step(carry, inp):
        c_st, n_st, m_st = carry
        qc, kc, vc, ic, fc = inp
        b = jnp.cumsum(fc, axis=1).transpose(0, 2, 1)
        ig = ic.transpose(0, 2, 1)
        g = b[..., -1]
        log_w = jnp.where(causal, b[..., :, None] - b[..., None, :] + ig[..., None, :], -jnp.inf)
        log_inter = b + m_st[..., None]
        m_t = jnp.maximum(log_inter, jnp.max(log_w, axis=-1))
        w = jnp.exp(log_w - m_t[..., None]) * jnp.einsum('blhd,bshd->bhls', qc, kc)
        s_inter = jnp.exp(log_inter - m_t)
        num = jnp.einsum('bhls,bshd->blhd', w, vc) + jnp.einsum('bhl,bhed,blhd->blhe', s_inter, c_st, qc)
        den = jnp.sum(w, axis=-1) + s_inter * jnp.einsum('bhd,blhd->bhl', n_st, qc)
        h = num / jnp.maximum(jnp.abs(den), jnp.exp(-m_t)).transpose(0, 2, 1)[..., None]
        log_end = g[..., None] - b + ig
        m_new = jnp.maximum(g + m_st, jnp.max(log_end, axis=-1))
        e = jnp.exp(log_end - m_new[..., None])
        decay = jnp.exp(g + m_st - m_new)
        c_new = decay[..., None, None] * c_st + jnp.einsum('bhs,bshe,bshd->bhed', e, vc, kc)
        n_new = decay[..., None] * n_st + jnp.einsum('bhs,bshd->bhd', e, kc)
        return (c_new, n_new, m_new), h

    state, hs = lax.scan(step, state, xs)
    return jnp.moveaxis(hs, 0, 1).reshape(bsz, t_len, n_h, dh), state


def _standardize(x):
    mu = jnp.mean(x, axis=-1, keepdims=True)
    var = jnp.mean(jnp.square(x - mu), axis=-1, keepdims=True)
    return (x - mu) * lax.rsqrt(var + LN_EPS)


def _even_core(streams_l, streams_c, p):
    ml_width = p['ml_norm_g'].shape[0]
    dh = ml_width // ML_HEADS

    def to_heads(a):
        return a.reshape(a.shape[:2] + (ML_HEADS, dh)).astype(F32)

    def prep(streams):
        rg_x, rg_gate, ml_x, ml_z = streams
        u = _depthwise_conv(rg_x, p['rg_conv_w'], p['rg_conv_b'], SHORT_PAD)
        ml_c = jax.nn.silu(_depthwise_conv(ml_x, p['ml_conv_w'], p['ml_conv_b'], SHORT_PAD))
        q, k, v = _block_diag(ml_c, p['ml_w_q']), _block_diag(ml_c, p['ml_w_k']), _block_diag(ml_x, p['ml_w_v'])
        qkv = jnp.concatenate([q, k, v], axis=-1)
        heads = (to_heads(q), to_heads(k) * dh ** -0.5, to_heads(v))
        return rg_gate, ml_z, u, ml_c, qkv, heads

    def gates(qkv, d):
        pre = (qkv @ p['ml_w_gate'][d] + p['ml_b_gate'][d]).astype(F32)
        return pre[..., :ML_HEADS], jax.nn.log_sigmoid(pre[..., ML_HEADS:])

    rgg_l, mlz_l, u_l, mlc_l, qkv_l, heads_l = prep(streams_l)
    rgg_c, mlz_c, u_c, mlc_c, qkv_c, heads_c = prep(streams_c)
    bsz = u_l.shape[0]
    rg_out_l, rg_out_c, ml_out_l, ml_out_c = [], [], [], []
    for d, rev in enumerate((False, True)):
        rg = [p[n][d] for n in ('rg_w_a', 'rg_b_a', 'rg_w_x', 'rg_b_x', 'rg_lambda')]
        h0 = jnp.zeros((bsz, u_l.shape[-1]), F32)
        hc = _rglru_direction(u_c, *rg, h0, rev)
        h_end = hc[:, 0] if rev else hc[:, -1]
        rg_out_c.append(hc)
        rg_out_l.append(_rglru_direction(u_l, *rg, h_end, rev))
        state0 = (jnp.zeros((bsz, ML_HEADS, dh, dh), F32), jnp.zeros((bsz, ML_HEADS, dh), F32),
                  jnp.zeros((bsz, ML_HEADS), F32))
        mc, st = _mlstm_chunkwise(*[_maybe_flip(a, rev) for a in heads_c + gates(qkv_c, d)], state0)
        ml, _ = _mlstm_chunkwise(*[_maybe_flip(a, rev) for a in heads_l + gates(qkv_l, d)], st)
        ml_out_c.append(_maybe_flip(mc, rev))
        ml_out_l.append(_maybe_flip(ml, rev))

    def finish(rg_gate, ml_z, ml_c, h_rg, h_ml):
        y_rg = h_rg * jax.nn.gelu(rg_gate)
        hn = _standardize(h_ml).reshape(h_ml.shape[:2] + (ml_width,)) * p['ml_norm_g']
        y_ml = (hn + p['ml_skip'] * ml_c) * jax.nn.silu(ml_z)
        return y_rg, y_ml

    parts_l = finish(rgg_l, mlz_l, mlc_l, rg_out_l[0] + rg_out_l[1], ml_out_l[0] + ml_out_l[1])
    parts_c = finish(rgg_c, mlz_c, mlc_c, rg_out_c[0] + rg_out_c[1], ml_out_c[0] + ml_out_c[1])
    return parts_l, parts_c


def _conformer_conv(glu, conv_w, conv_b, ln_g, ln_b):
    y = _depthwise_conv(glu, conv_w, conv_b, CV_PAD)
    return jax.nn.silu(_standardize(y) * ln_g + ln_b)


def kernel(x, c, ctx, c_ctx, w_ada, b_ada, ln_g, ln_b, ffn_w_gate, ffn_w_up, ffn_w_down, ev_w_in, ev_w_out, rg_conv_w, rg_conv_b, rg_w_a, rg_b_a, rg_w_x, rg_b_x, rg_lambda, ml_conv_w, ml_conv_b, ml_w_q, ml_w_k, ml_w_v, ml_w_gate, ml_b_gate, ml_norm_g, ml_skip, od_w_in, od_w_out, cv_conv_w, cv_conv_b, cv_ln_g, cv_ln_b, da_lambda, da_subln_g):
    depth = w_ada.shape[0]
    d = x.shape[-1]
    alpha = (2 * depth) ** 0.25
    bsz, t_len = x.shape[:2]
    ctx_len = ctx.shape[1]
    rope = _rope_tables(t_len)
    x_lat, x_ctx = x, ctx
    cond_lat = jax.nn.silu(c)
    cond_ctx = jax.nn.silu(c_ctx)
    cw = cv_conv_w.shape[-1]
    dw = da_subln_g.shape[-1] * DA_HEADS
    for layer in range(depth):
        need_ctx = layer < depth - 1
        mod_l = (cond_lat @ w_ada[layer] + b_ada[layer]).reshape(bsz, N_SUB, 3, d)
        mod_c = (cond_ctx @ w_ada[layer] + b_ada[layer]).reshape(1, N_SUB, 3, d)
        ln = jnp.stack([ln_g[layer], ln_b[layer]], axis=1)
        ffn1 = _prep_ffn(ffn_w_gate[layer, 0], ffn_w_up[layer, 0], ffn_w_down[layer, 0])
        ffn2 = _prep_ffn(ffn_w_gate[layer, 1], ffn_w_up[layer, 1], ffn_w_down[layer, 1])
        x_lat = ffn_sublayer(x_lat, mod_l[:, 0], *ffn1, ln[0], alpha)
        x_ctx = ffn_sublayer(x_ctx, mod_c[:, 0], *ffn1, ln[0], alpha)
        i = layer // 2
        if layer % 2 == 0:
            w_in = ev_w_in[i].astype(BF16)
            p = dict(rg_conv_w=rg_conv_w[i], rg_conv_b=rg_conv_b[i], rg_w_a=rg_w_a[i], rg_b_a=rg_b_a[i],
                     rg_w_x=rg_w_x[i], rg_b_x=rg_b_x[i], rg_lambda=rg_lambda[i], ml_conv_w=ml_conv_w[i],
                     ml_conv_b=ml_conv_b[i], ml_w_q=ml_w_q[i], ml_w_k=ml_w_k[i], ml_w_v=ml_w_v[i],
                     ml_w_gate=ml_w_gate[i], ml_b_gate=ml_b_gate[i], ml_norm_g=ml_norm_g[i], ml_skip=ml_skip[i])
            parts_l, parts_c = _even_core(inproj_even(x_lat, mod_l[:, 1], w_in),
                                          inproj_even(x_ctx, mod_c[:, 1], w_in), p)
            w_out = ev_w_out[i].astype(BF16)
            w_parts = (w_out[:parts_l[0].shape[-1]], w_out[parts_l[0].shape[-1]:])
        else:
            w_in = od_w_in[i].astype(BF16)
            lam_init = 0.8 - 0.6 * math.exp(-0.3 * layer)
            lamf = da_lambda[i].astype(F32)
            lam = (jnp.exp(jnp.sum(lamf[0] * lamf[1])) - jnp.exp(jnp.sum(lamf[2] * lamf[3])) + lam_init).reshape(1)
            subln = da_subln_g[i].reshape(1, -1)
            glu_l, q_l, k_l, v_l = inproj_odd(x_lat, mod_l[:, 1], w_in, rope, cw, dw)
            glu_c, q_c, k_c, v_c = inproj_odd(x_ctx, mod_c[:, 1], w_in, None, cw, dw)
            k_all = jnp.concatenate([k_c, k_l], axis=1)
            v_all = jnp.concatenate([v_c, v_l], axis=1)
            attn_l = diff_attention(q_l, k_all, v_all, lam, subln, 1.0 - lam_init, 512, ctx_len * 3)
            conv = (cv_conv_w[i], cv_conv_b[i], cv_ln_g[i], cv_ln_b[i])
            parts_l = (_conformer_conv(glu_l, *conv), attn_l)
            if need_ctx:
                attn_c = diff_attention(q_c, k_c, v_c, lam, subln, 1.0 - lam_init, ctx_len, ctx_len)
                parts_c = (_conformer_conv(glu_c, *conv), attn_c)
            w_out = od_w_out[i].astype(BF16)
            w_parts = (w_out[:cw], w_out[cw:])
        x_lat = mixer_out(x_lat, mod_l[:, 1], ln[1], parts_l, w_parts, alpha)
        x_lat = ffn_sublayer(x_lat, mod_l[:, 2], *ffn2, ln[2], alpha)
        if need_ctx:
            x_ctx = mixer_out(x_ctx, mod_c[:, 1], ln[1], parts_c, w_parts, alpha)
            x_ctx = ffn_sublayer(x_ctx, mod_c[:, 2], *ffn2, ln[2], alpha)
    return x_lat
```

```python
import functools
import math

import jax
import jax.numpy as jnp
from jax import lax
from jax.experimental import pallas as pl
from jax.experimental.pallas import tpu as pltpu

F32 = jnp.float32
BF16 = jnp.bfloat16

GRID_W = 64
N_SUB = 3
HALF_STEP = 0.5
LN_EPS = 1e-5
RG_C = 8.0
ML_HEADS = 4
DA_HEADS = 8
DA_HEAD_DIM = 64
DA_V_DIM = 2 * DA_HEAD_DIM
ROPE_BASE = 10000.0

FFN_CHUNK = 256
MXU_TILE = 256
ML_CHUNK = 256
HALO = 8
CV_HALO = 16
CV_ROWS = 64
VMEM_LIMIT = 56 * 1024 * 1024


def _params(*sem):
    return pltpu.CompilerParams(dimension_semantics=sem, vmem_limit_bytes=VMEM_LIMIT)


def _row_tile(t, want):
    tm = min(t, want)
    assert t % tm == 0
    return tm


def _layer_norm(z, g, b):
    mu = jnp.mean(z, axis=-1, keepdims=True)
    zc = z - mu
    var = jnp.mean(zc * zc, axis=-1, keepdims=True)
    return zc * lax.rsqrt(var + LN_EPS) * g + b


def _softplus(z):
    return jnp.maximum(z, 0.0) + jnp.log1p(jnp.exp(-jnp.abs(z)))


def _dot(a, b):
    return jnp.dot(a, b, preferred_element_type=F32)


def _dot_nt(a, b):
    return lax.dot_general(a, b, (((1,), (1,)), ((), ())), preferred_element_type=F32)


def _mod_spec(mod):
    per_sample = mod.shape[0] > 1
    return pl.BlockSpec((1,) + mod.shape[1:], lambda b, i: (b if per_sample else 0, 0, 0))


def _const_spec(shape):
    return pl.BlockSpec(shape, lambda b, i: (0,) * len(shape))


def _halo_specs(t, tm, width, halo, time_index):
    per_tile, n_halo = tm // halo, t // halo
    cur = pl.BlockSpec((1, tm, width), lambda b, i: (b, time_index(i), 0))
    prev = pl.BlockSpec((1, halo, width), lambda b, i: (b, jnp.maximum(time_index(i) * per_tile - 1, 0), 0))
    nxt = pl.BlockSpec((1, halo, width),
                       lambda b, i: (b, jnp.minimum((time_index(i) + 1) * per_tile, n_halo - 1), 0))
    return [cur, prev, nxt]


def _fill_padded(xe_sc, cur, prev_ref, next_ref, ti, n_tiles, halo):
    tm = cur.shape[0]
    xe_sc[0:halo] = prev_ref[0] * (ti > 0).astype(F32)
    xe_sc[halo:halo + tm] = cur
    xe_sc[halo + tm:2 * halo + tm] = next_ref[0] * (ti < n_tiles - 1).astype(F32)


def _short_conv(xe_sc, w_ref, b_ref, tm):
    y = b_ref[...] + w_ref[0:1] * xe_sc[pl.ds(HALO - 1, tm), :]
    for j in range(1, w_ref.shape[0]):
        y = y + w_ref[j:j + 1] * xe_sc[pl.ds(HALO - 1 + j, tm), :]
    return y


def _ffn_body(x_ref, mod_ref, wg_ref, wu_ref, wd_ref, ln_ref, o_ref, *, alpha):
    x = x_ref[0]
    mod = mod_ref[0]
    h = (x * (1.0 + mod[1:2]) + mod[0:1]).astype(BF16)
    y = jnp.zeros(x.shape, F32)
    for c in range(wg_ref.shape[0]):
        g = _dot(h, wg_ref[c])
        u = _dot(h, wu_ref[c])
        a = (g * jax.nn.sigmoid(g) * u).astype(BF16)
        y = y + _dot(a, wd_ref[c])
    z = alpha * x + (HALF_STEP * mod[2:3]) * y
    o_ref[0] = _layer_norm(z, ln_ref[0:1], ln_ref[1:2])


def ffn_sublayer(x, mod, wg, wu, wd, ln, alpha):
    bsz, t, d = x.shape
    tm = _row_tile(t, 512)
    return pl.pallas_call(
        functools.partial(_ffn_body, alpha=alpha),
        grid=(bsz, t // tm),
        in_specs=[pl.BlockSpec((1, tm, d), lambda b, i: (b, i, 0)), _mod_spec(mod),
                  _const_spec(wg.shape), _const_spec(wu.shape), _const_spec(wd.shape), _const_spec(ln.shape)],
        out_specs=pl.BlockSpec((1, tm, d), lambda b, i: (b, i, 0)),
        out_shape=jax.ShapeDtypeStruct(x.shape, F32),
        compiler_params=_params("parallel", "parallel"),
        name="ffn_sublayer",
    )(x, mod, wg, wu, wd, ln)


def _prep_ffn(wg, wu, wd):
    d, f = wg.shape
    n = f // FFN_CHUNK
    to_chunks = lambda w: w.astype(BF16).reshape(d, n, FFN_CHUNK).transpose(1, 0, 2)
    return to_chunks(wg), to_chunks(wu), wd.astype(BF16).reshape(n, FFN_CHUNK, d)


def _inproj_even_body(x_ref, mod_ref, w_ref, *o_refs):
    mod = mod_ref[0]
    h = (x_ref[0] * (1.0 + mod[1:2]) + mod[0:1]).astype(BF16)
    width = o_refs[0].shape[-1]
    for j, o_ref in enumerate(o_refs):
        o_ref[0] = _dot(h, w_ref[:, j * width:(j + 1) * width])


def inproj_even(x, mod, w_in):
    bsz, t, d = x.shape
    tm = _row_tile(t, 512)
    n_out = w_in.shape[1] // d
    row_spec = pl.BlockSpec((1, tm, d), lambda b, i: (b, i, 0))
    return pl.pallas_call(
        _inproj_even_body,
        grid=(bsz, t // tm),
        in_specs=[row_spec, _mod_spec(mod), _const_spec(w_in.shape)],
        out_specs=[row_spec] * n_out,
        out_shape=[jax.ShapeDtypeStruct(x.shape, F32)] * n_out,
        compiler_params=_params("parallel", "parallel"),
        name="inproj_even",
    )(x, mod, w_in)


def _rope128(a, cos, sin_lo, sin_hi):
    return a * cos + pltpu.roll(a, 112, 1) * sin_lo + pltpu.roll(a, 16, 1) * sin_hi


def _inproj_odd_body(x_ref, mod_ref, w_ref, *rest, use_rope):
    if use_rope:
        cos_ref, slo_ref, shi_ref, glu_ref, q_ref, k_ref, v_ref = rest
    else:
        glu_ref, q_ref, k_ref, v_ref = rest
    mod = mod_ref[0]
    h = (x_ref[0] * (1.0 + mod[1:2]) + mod[0:1]).astype(BF16)
    cw = glu_ref.shape[-1]
    a = _dot(h, w_ref[:, :cw])
    gate = _dot(h, w_ref[:, cw:2 * cw])
    glu_ref[0] = a * jax.nn.sigmoid(gate)
    dw = q_ref.shape[-1]
    q = _dot(h, w_ref[:, 2 * cw:2 * cw + dw]) * (DA_HEAD_DIM ** -0.5)
    k = _dot(h, w_ref[:, 2 * cw + dw:2 * cw + 2 * dw])
    if use_rope:
        cos, slo, shi = cos_ref[...], slo_ref[...], shi_ref[...]
        for j in range(dw // 128):
            sl = slice(128 * j, 128 * (j + 1))
            q_ref[0, :, sl] = _rope128(q[:, sl], cos, slo, shi).astype(BF16)
            k_ref[0, :, sl] = _rope128(k[:, sl], cos, slo, shi).astype(BF16)
    else:
        q_ref[0] = q.astype(BF16)
        k_ref[0] = k.astype(BF16)
    v_ref[0] = _dot(h, w_ref[:, 2 * cw + 2 * dw:]).astype(BF16)


def inproj_odd(x, mod, w_in, rope, cw, dw):
    bsz, t, d = x.shape
    tm = _row_tile(t, 512)
    row = lambda w: pl.BlockSpec((1, tm, w), lambda b, i: (b, i, 0))
    in_specs = [row(d), _mod_spec(mod), _const_spec(w_in.shape)]
    args = [x, mod, w_in]
    if rope is not None:
        in_specs += [pl.BlockSpec((tm, 128), lambda b, i: (i, 0))] * 3
        args += list(rope)
    return pl.pallas_call(
        functools.partial(_inproj_odd_body, use_rope=rope is not None),
        grid=(bsz, t // tm),
        in_specs=in_specs,
        out_specs=[row(cw), row(dw), row(dw), row(dw)],
        out_shape=[jax.ShapeDtypeStruct((bsz, t, cw), F32)] + [jax.ShapeDtypeStruct((bsz, t, dw), BF16)] * 3,
        compiler_params=_params("parallel", "parallel"),
        name="inproj_odd",
    )(*args)


def _rope_tables(t):
    pos = jnp.arange(t)
    row = (pos // GRID_W).astype(F32)
    col = (pos % GRID_W).astype(F32)
    axis_dim = DA_HEAD_DIM // 2
    inv_freq = ROPE_BASE ** (-jnp.arange(0, axis_dim, 2, dtype=F32) / axis_dim)
    ang_r = row[:, None] * inv_freq
    ang_c = col[:, None] * inv_freq
    zeros = jnp.zeros_like(ang_r)
    cos64 = jnp.concatenate([jnp.cos(ang_r)] * 2 + [jnp.cos(ang_c)] * 2, axis=-1)
    sin_lo64 = jnp.concatenate([-jnp.sin(ang_r), zeros, -jnp.sin(ang_c), zeros], axis=-1)
    sin_hi64 = jnp.concatenate([zeros, jnp.sin(ang_r), zeros, jnp.sin(ang_c)], axis=-1)
    return tuple(jnp.tile(a, (1, 2)) for a in (cos64, sin_lo64, sin_hi64))


def _attn_body(lam_ref, q_ref, k_ref, v_ref, g_ref, o_ref, m_sc, l_sc, acc_sc, *, out_scale):
    j = pl.program_id(3)

    @pl.when(j == 0)
    def _():
        m_sc[...] = jnp.full(m_sc.shape, -jnp.inf, F32)
        l_sc[...] = jnp.zeros(l_sc.shape, F32)
        acc_sc[...] = jnp.zeros(acc_sc.shape, F32)

    q, k, v = q_ref[0], k_ref[0], v_ref[0]
    for m in range(2):
        sl = slice(DA_HEAD_DIM * m, DA_HEAD_DIM * (m + 1))
        s = _dot_nt(q[:, sl], k[:, sl])
        m_prev = m_sc[m]
        m_new = jnp.maximum(m_prev, jnp.max(s, axis=-1, keepdims=True))
        p = jnp.exp(s - m_new)
        corr = jnp.exp(m_prev - m_new)
        l_sc[m] = corr * l_sc[m] + jnp.sum(p, axis=-1, keepdims=True)
        acc_sc[m] = corr * acc_sc[m] + _dot(p.astype(BF16), v)
        m_sc[m] = m_new

    @pl.when(j == pl.num_programs(3) - 1)
    def _():
        o = acc_sc[0] / l_sc[0] - lam_ref[0] * (acc_sc[1] / l_sc[1])
        o = o * lax.rsqrt(jnp.mean(o * o, axis=-1, keepdims=True) + LN_EPS)
        o_ref[0] = o * g_ref[...] * out_scale


def diff_attention(q, k, v, lam, subln_g, out_scale, tq, tk):
    bsz, t, dw = q.shape
    s_len = k.shape[1]
    n_heads = dw // DA_V_DIM
    assert t % tq == 0 and s_len % tk == 0
    return pl.pallas_call(
        functools.partial(_attn_body, out_scale=out_scale),
        grid=(bsz, n_heads, t // tq, s_len // tk),
        in_specs=[pl.BlockSpec(memory_space=pltpu.SMEM),
                  pl.BlockSpec((1, tq, DA_V_DIM), lambda b, h, i, j: (b, i, h)),
                  pl.BlockSpec((1, tk, DA_V_DIM), lambda b, h, i, j: (b, j, h)),
                  pl.BlockSpec((1, tk, DA_V_DIM), lambda b, h, i, j: (b, j, h)),
                  pl.BlockSpec((1, DA_V_DIM), lambda b, h, i, j: (0, 0))],
        out_specs=pl.BlockSpec((1, tq, DA_V_DIM), lambda b, h, i, j: (b, i, h)),
        out_shape=jax.ShapeDtypeStruct((bsz, t, dw), F32),
        scratch_shapes=[pltpu.VMEM((2, tq, 1), F32), pltpu.VMEM((2, tq, 1), F32),
                        pltpu.VMEM((2, tq, DA_V_DIM), F32)],
        compiler_params=_params("parallel", "parallel", "parallel", "arbitrary"),
        name="diff_attention",
    )(lam, q, k, v, subln_g)


def _cvconv_body(x_ref, prev_ref, next_ref, w_ref, b_ref, g_ref, beta_ref, o_ref, xe_sc):
    i, n = pl.program_id(1), pl.num_programs(1)
    tm = x_ref.shape[1]
    _fill_padded(xe_sc, x_ref[0], prev_ref, next_ref, i, n, CV_HALO)
    n_taps = w_ref.shape[0]
    first = CV_HALO - (n_taps - 1) // 2

    for base in range(0, tm, CV_ROWS):
        y = b_ref[...] + w_ref[0:1] * xe_sc[base + first:base + first + CV_ROWS, :]
        for j in range(1, n_taps):
            y = y + w_ref[j:j + 1] * xe_sc[base + first + j:base + first + j + CV_ROWS, :]
        y = _layer_norm(y, g_ref[...], beta_ref[...])
        o_ref[0, base:base + CV_ROWS, :] = y * jax.nn.sigmoid(y)


def conformer_conv(glu, conv_w, conv_b, ln_g, ln_b):
    bsz, t, cw = glu.shape
    tm = _row_tile(t, 512)
    vec = lambda a: a.reshape(1, cw)
    return pl.pallas_call(
        _cvconv_body,
        grid=(bsz, t // tm),
        in_specs=_halo_specs(t, tm, cw, CV_HALO, lambda i: i)
                 + [_const_spec(conv_w.shape)] + [_const_spec((1, cw))] * 3,
        out_specs=pl.BlockSpec((1, tm, cw), lambda b, i: (b, i, 0)),
        out_shape=jax.ShapeDtypeStruct(glu.shape, F32),
        scratch_shapes=[pltpu.VMEM((tm + 2 * CV_HALO, cw), F32)],
        compiler_params=_params("parallel", "parallel"),
        name="conformer_conv",
    )(glu, glu, glu, conv_w, vec(conv_b), vec(ln_g), vec(ln_b))


def _rglru_body(x_ref, prev_ref, next_ref, cw_ref, cb_ref, wa_ref, ba_ref, wx_ref, bx_ref, lam_ref, h0_ref,
                h_ref, hlast_ref, xe_sc, a_sc, b_sc, state_sc, *, reverse):
    i, n = pl.program_id(1), pl.num_programs(1)
    ti = n - 1 - i if reverse else i
    tm = x_ref.shape[1]

    @pl.when(i == 0)
    def _():
        state_sc[...] = h0_ref[0]

    _fill_padded(xe_sc, x_ref[0], prev_ref, next_ref, ti, n, HALO)
    u = _short_conv(xe_sc, cw_ref, cb_ref, tm)
    ub = u.astype(BF16)
    for c in range(wa_ref.shape[0]):
        sl = slice(MXU_TILE * c, MXU_TILE * (c + 1))
        r = jax.nn.sigmoid(_dot(ub[:, sl], wa_ref[c]) + ba_ref[:, sl])
        gate_in = jax.nn.sigmoid(_dot(ub[:, sl], wx_ref[c]) + bx_ref[:, sl])
        log_a = (-RG_C) * r * _softplus(-lam_ref[:, sl])
        a = jnp.exp(log_a)
        a_sc[:, sl] = a
        b_sc[:, sl] = jnp.sqrt(-jnp.tanh(log_a) * (a * a + 1.0)) * gate_in * u[:, sl]

    def step(s, h):
        row = tm - 1 - s if reverse else s
        h = a_sc[pl.ds(row, 1), :] * h + b_sc[pl.ds(row, 1), :]
        h_ref[0, pl.ds(row, 1), :] = h
        return h

    h = lax.fori_loop(0, tm, step, state_sc[...], unroll=8)
    state_sc[...] = h

    @pl.when(i == n - 1)
    def _():
        hlast_ref[0] = h


def rglru_direction(rg_x, conv_w, conv_b, w_a, b_a, w_x, b_x, lam, h0, reverse):
    bsz, t, width = rg_x.shape
    tm = _row_tile(t, 512)
    n = t // tm
    tidx = (lambda i: n - 1 - i) if reverse else (lambda i: i)
    vec = _const_spec((1, width))
    state_spec = pl.BlockSpec((1, 1, width), lambda b, i: (b, 0, 0))
    return pl.pallas_call(
        functools.partial(_rglru_body, reverse=reverse),
        grid=(bsz, n),
        in_specs=_halo_specs(t, tm, width, HALO, tidx)
                 + [_const_spec(conv_w.shape), vec, _const_spec(w_a.shape), vec, _const_spec(w_x.shape), vec, vec,
                    state_spec],
        out_specs=[pl.BlockSpec((1, tm, width), lambda b, i: (b, tidx(i), 0)), state_spec],
        out_shape=[jax.ShapeDtypeStruct(rg_x.shape, F32), jax.ShapeDtypeStruct((bsz, 1, width), F32)],
        scratch_shapes=[pltpu.VMEM((tm + 2 * HALO, width), F32), pltpu.VMEM((tm, width), F32),
                        pltpu.VMEM((tm, width), F32), pltpu.VMEM((1, width), F32)],
        compiler_params=_params("parallel", "arbitrary"),
        name="rglru_rev" if reverse else "rglru_fwd",
    )(rg_x, rg_x, rg_x, conv_w, conv_b, w_a, b_a, w_x, b_x, lam, h0)


def _split3(a):
    hi = a.astype(BF16)
    rest = a - hi.astype(F32)
    mid = rest.astype(BF16)
    return hi, mid, (rest - mid.astype(F32)).astype(BF16)


def _mlprep_body(x_ref, prev_ref, next_ref, cw_ref, cb_ref, wq_ref, wk_ref, wv_ref, wvt_ref, wg_ref, wgt_ref,
                 bg_ref, bgt_ref, tril_ref, triu_ref,
                 q_ref, k_ref, v_ref, vt_ref, mlc_ref, gcol_ref, grow_ref, xe_sc, *, k_scale):
    i, n = pl.program_id(1), pl.num_programs(1)
    tm = x_ref.shape[1]
    x = x_ref[0]
    _fill_padded(xe_sc, x, prev_ref, next_ref, i, n, HALO)
    conv = _short_conv(xe_sc, cw_ref, cb_ref, tm)
    ml_c = conv * jax.nn.sigmoid(conv)
    mlc_ref[0] = ml_c
    cb16, xb16 = ml_c.astype(BF16), x.astype(BF16)
    width = x.shape[1]
    n_gate = wg_ref.shape[1]
    pre = jnp.zeros((tm, n_gate), F32) + bg_ref[...]
    pre_t = jnp.zeros((n_gate, tm), F32) + bgt_ref[...]
    for c in range(wq_ref.shape[0]):
        sl = slice(MXU_TILE * c, MXU_TILE * (c + 1))
        q = _dot(cb16[:, sl], wq_ref[c]).astype(BF16)
        k = _dot(cb16[:, sl], wk_ref[c]).astype(BF16)
        v = _dot(xb16[:, sl], wv_ref[c]).astype(BF16)
        q_ref[0, :, sl] = q
        k_ref[0, :, sl] = k * k_scale
        v_ref[0, :, sl] = v
        vt_ref[0, sl, :] = _dot_nt(wvt_ref[c], xb16[:, sl]).astype(BF16)
        for part, val in enumerate((q, k, v)):
            rows = slice(part * width + MXU_TILE * c, part * width + MXU_TILE * (c + 1))
            pre = pre + _dot(val, wg_ref[rows, :])
            pre_t = pre_t + _dot_nt(wgt_ref[:, rows], val)
    half = n_gate // 2
    col = lax.broadcasted_iota(jnp.int32, (ML_CHUNK, n_gate), 1)
    row = lax.broadcasted_iota(jnp.int32, (n_gate, ML_CHUNK), 0)
    for c in range(tm // ML_CHUNK):
        ch = slice(ML_CHUNK * c, ML_CHUNK * (c + 1))
        pre_c, pre_tc = pre[ch], pre_t[:, ch]
        logf_c = jnp.where(col >= half, -_softplus(-pre_c), 0.0)
        logf_tc = jnp.where(row >= half, -_softplus(-pre_tc), 0.0)
        pieces, pieces_t = _split3(logf_c), _split3(logf_tc)
        prefix = sum(_dot(tril_ref[...], p) for p in pieces)
        suffix = sum(_dot(triu_ref[...], p) for p in pieces)
        prefix_t = sum(_dot(p, triu_ref[...]) for p in pieces_t)
        suffix_t = sum(_dot(p, tril_ref[...]) for p in pieces_t)
        gcol_ref[0, ch, :] = jnp.where(col < half, pre_c, jnp.where(col < half + half // 2, prefix, suffix))
        grow_ref[0, :, ch] = jnp.where(row < half, pre_tc, jnp.where(row < half + half // 2, prefix_t, suffix_t))


def mlstm_prep(ml_x, conv_w, conv_b, wq, wk, wv, wvt, wg, wgt, bg, bgt, tril, triu, k_scale):
    bsz, t, width = ml_x.shape
    tm = _row_tile(t, 512)
    n_gate = wg.shape[1]
    row = pl.BlockSpec((1, tm, width), lambda b, i: (b, i, 0))
    consts = (conv_w, conv_b, wq, wk, wv, wvt, wg, wgt, bg, bgt, tril, triu)
    return pl.pallas_call(
        functools.partial(_mlprep_body, k_scale=k_scale),
        grid=(bsz, t // tm),
        in_specs=_halo_specs(t, tm, width, HALO, lambda i: i) + [_const_spec(a.shape) for a in consts],
        out_specs=[row, row, row, pl.BlockSpec((1, width, tm), lambda b, i: (b, 0, i)), row,
                   pl.BlockSpec((1, tm, n_gate), lambda b, i: (b, i, 0)),
                   pl.BlockSpec((1, n_gate, tm), lambda b, i: (b, 0, i))],
        out_shape=[jax.ShapeDtypeStruct(ml_x.shape, BF16)] * 3
                  + [jax.ShapeDtypeStruct((bsz, width, t), BF16), jax.ShapeDtypeStruct(ml_x.shape, F32),
                     jax.ShapeDtypeStruct((bsz, t, n_gate), F32), jax.ShapeDtypeStruct((bsz, n_gate, t), F32)],
        scratch_shapes=[pltpu.VMEM((tm + 2 * HALO, width), F32)],
        compiler_params=_params("parallel", "parallel"),
        name="mlstm_prep",
    )(ml_x, ml_x, ml_x, *consts)


def _mlstm_body(q_ref, k_ref, v_ref, vt_ref, gcol_ref, grow_ref, c0_ref, n0_ref, m0_ref,
                h_ref, cf_ref, nf_ref, mf_ref, c_sc, n_sc, m_sc, *, reverse):
    i, n = pl.program_id(1), pl.num_programs(1)
    length = q_ref.shape[1]
    n_heads = c_sc.shape[0]
    dh = c_sc.shape[1]

    @pl.when(i == 0)
    def _():
        c_sc[...] = c0_ref[0]
        n_sc[...] = n0_ref[0]
        m_sc[...] = m0_ref[0]

    direction = 1 if reverse else 0
    t_idx = lax.broadcasted_iota(jnp.int32, (length, length), 0)
    s_idx = lax.broadcasted_iota(jnp.int32, (length, length), 1)
    visible = (s_idx >= t_idx) if reverse else (s_idx <= t_idx)
    for h in range(n_heads):
        sl = slice(dh * h, dh * (h + 1))
        ci = n_heads * direction + h
        cb = 2 * n_heads + ci
        qh, kh, vh, vth = q_ref[0, :, sl], k_ref[0, :, sl], v_ref[0, :, sl], vt_ref[0, sl, :]
        ig_row, b_row = grow_ref[0, ci:ci + 1, :], grow_ref[0, cb:cb + 1, :]
        b_col = gcol_ref[0, :, cb:cb + 1]
        g = b_col[0:1] if reverse else b_col[length - 1:length]
        m_st = m_sc[h]
        log_w = jnp.where(visible, b_col - b_row + ig_row, -jnp.inf)
        log_inter = b_col + m_st
        m_t = jnp.maximum(log_inter, jnp.max(log_w, axis=-1, keepdims=True))
        w = jnp.exp(log_w - m_t) * _dot_nt(qh, kh)
        s_inter = jnp.exp(log_inter - m_t)
        num = _dot(w.astype(BF16), vh) + s_inter * _dot_nt(qh, c_sc[h].astype(BF16))
        den = jnp.sum(w, axis=-1, keepdims=True) \
            + s_inter * jnp.sum(qh.astype(F32) * n_sc[h], axis=-1, keepdims=True)
        h_ref[0, :, sl] = num / jnp.maximum(jnp.abs(den), jnp.exp(-m_t))
        log_end = g - b_row + ig_row
        m_new = jnp.maximum(g + m_st, jnp.max(log_end, axis=-1, keepdims=True))
        e_row = jnp.exp(log_end - m_new)
        decay = jnp.exp(g + m_st - m_new)
        c_sc[h] = decay * c_sc[h] + _dot((vth * e_row).astype(BF16), kh)
        e8 = jnp.broadcast_to(e_row, (8, length)).astype(BF16)
        n_sc[h] = decay * n_sc[h] + _dot(e8, kh)[0:1]
        m_sc[h] = m_new

    @pl.when(i == n - 1)
    def _():
        cf_ref[0] = c_sc[...]
        nf_ref[0] = n_sc[...]
        mf_ref[0] = m_sc[...]


def mlstm_direction(q, k, v, vt, gcol, grow, state, reverse):
    bsz, t, width = q.shape
    n = t // ML_CHUNK
    n_gate = gcol.shape[-1]
    dh = width // ML_HEADS
    tidx = (lambda i: n - 1 - i) if reverse else (lambda i: i)
    row = pl.BlockSpec((1, ML_CHUNK, width), lambda b, i: (b, tidx(i), 0))
    state_specs = [pl.BlockSpec((1,) + s.shape[1:], lambda b, i: (b, 0, 0, 0)) for s in state]
    outs = pl.pallas_call(
        functools.partial(_mlstm_body, reverse=reverse),
        grid=(bsz, n),
        in_specs=[row, row, row, pl.BlockSpec((1, width, ML_CHUNK), lambda b, i: (b, 0, tidx(i))),
                  pl.BlockSpec((1, ML_CHUNK, n_gate), lambda b, i: (b, tidx(i), 0)),
                  pl.BlockSpec((1, n_gate, ML_CHUNK), lambda b, i: (b, 0, tidx(i)))] + state_specs,
        out_specs=[row] + state_specs,
        out_shape=[jax.ShapeDtypeStruct(q.shape, F32)] + [jax.ShapeDtypeStruct(s.shape, F32) for s in state],
        scratch_shapes=[pltpu.VMEM((ML_HEADS, dh, dh), F32), pltpu.VMEM((ML_HEADS, 1, dh), F32),
                        pltpu.VMEM((ML_HEADS, 1, 1), F32)],
        compiler_params=_params("parallel", "arbitrary"),
        name="mlstm_rev" if reverse else "mlstm_fwd",
    )(q, k, v, vt, gcol, grow, *state)
    return outs[0], tuple(outs[1:])


def _mixout_body(*refs, n_parts, alpha):
    x_ref, mod_ref, ln_ref = refs[:3]
    part_refs, w_refs, o_ref = refs[3:3 + n_parts], refs[3 + n_parts:3 + 2 * n_parts], refs[-1]
    x = x_ref[0]
    y = jnp.zeros(x.shape, F32)
    for p_ref, w_ref in zip(part_refs, w_refs):
        y = y + _dot(p_ref[0].astype(BF16), w_ref[...])
    z = alpha * x + mod_ref[0][2:3] * y
    o_ref[0] = _layer_norm(z, ln_ref[0:1], ln_ref[1:2])


def mixer_out(x, mod, ln, parts, weights, alpha):
    bsz, t, d = x.shape
    tm = _row_tile(t, 512)
    row = lambda w: pl.BlockSpec((1, tm, w), lambda b, i: (b, i, 0))
    return pl.pallas_call(
        functools.partial(_mixout_body, n_parts=len(parts), alpha=alpha),
        grid=(bsz, t // tm),
        in_specs=[row(d), _mod_spec(mod), _const_spec(ln.shape)] + [row(p.shape[-1]) for p in parts]
                 + [_const_spec(w.shape) for w in weights],
        out_specs=row(d),
        out_shape=jax.ShapeDtypeStruct(x.shape, F32),
        compiler_params=_params("parallel", "parallel"),
        name="mixer_out",
    )(x, mod, ln, *parts, *weights)


def _evenout_body(x_ref, mod_ref, ln_ref, rgf_ref, rgr_ref, gate_ref, mlf_ref, mlr_ref, mlc_ref, mlz_ref,
                  ng_ref, skip_ref, wrg_ref, wml_ref, o_ref, *, alpha):
    x = x_ref[0]
    y_rg = (rgf_ref[0] + rgr_ref[0]) * jax.nn.gelu(gate_ref[0], approximate=True)
    y = _dot(y_rg.astype(BF16), wrg_ref[...])
    h_ml = mlf_ref[0] + mlr_ref[0]
    z_gate = mlz_ref[0]
    dh = h_ml.shape[1] // ML_HEADS
    for h in range(ML_HEADS):
        sl = slice(dh * h, dh * (h + 1))
        hh = h_ml[:, sl]
        mu = jnp.mean(hh, axis=-1, keepdims=True)
        hc = hh - mu
        var = jnp.mean(hc * hc, axis=-1, keepdims=True)
        hn = hc * lax.rsqrt(var + LN_EPS) * ng_ref[:, sl]
        zg = z_gate[:, sl]
        y_ml = (hn + skip_ref[:, sl] * mlc_ref[0, :, sl]) * (zg * jax.nn.sigmoid(zg))
        y = y + _dot(y_ml.astype(BF16), wml_ref[sl, :])
    z = alpha * x + mod_ref[0][2:3] * y
    o_ref[0] = _layer_norm(z, ln_ref[0:1], ln_ref[1:2])


def even_out(x, mod, ln, streams, norm_g, skip, w_rg, w_ml, alpha):
    bsz, t, d = x.shape
    tm = _row_tile(t, 256)
    row = pl.BlockSpec((1, tm, d), lambda b, i: (b, i, 0))
    vec = _const_spec((1, d))
    return pl.pallas_call(
        functools.partial(_evenout_body, alpha=alpha),
        grid=(bsz, t // tm),
        in_specs=[row, _mod_spec(mod), _const_spec(ln.shape)] + [row] * len(streams)
                 + [vec, vec, _const_spec(w_rg.shape), _const_spec(w_ml.shape)],
        out_specs=row,
        out_shape=jax.ShapeDtypeStruct(x.shape, F32),
        compiler_params=_params("parallel", "parallel"),
        name="even_out",
    )(x, mod, ln, *streams, norm_g, skip, w_rg, w_ml)


def _dense_blocks(w, group):
    g, di, do = w.shape
    wb = w.reshape(g // group, group, di, do)
    eye = jnp.eye(group, dtype=w.dtype)
    return jnp.einsum('cgio,gh->cgiho', wb, eye).reshape(g // group, group * di, group * do).astype(BF16)


def _even_mixer(x_lat, x_ctx, mod_l, mod_c, ln, alpha, p):
    d = x_lat.shape[-1]
    bsz = x_lat.shape[0]
    w_in = p['w_in'].astype(BF16)
    streams = [inproj_even(x_lat, mod_l, w_in), inproj_even(x_ctx, mod_c, w_in)]
    width = streams[0][0].shape[-1]
    dh = width // ML_HEADS
    vec = lambda a: a.reshape(1, -1)
    rg_out = [[], []]
    for dr, rev in enumerate((False, True)):
        consts = (p['rg_conv_w'], vec(p['rg_conv_b']), _dense_blocks(p['rg_w_a'][dr], MXU_TILE // p['rg_w_a'].shape[-1]),
                  vec(p['rg_b_a'][dr]), _dense_blocks(p['rg_w_x'][dr], MXU_TILE // p['rg_w_x'].shape[-1]),
                  vec(p['rg_b_x'][dr]), vec(p['rg_lambda'][dr]))
        hc, h_end = rglru_direction(streams[1][0], *consts, jnp.zeros((bsz, 1, width), F32), rev)
        hl, _ = rglru_direction(streams[0][0], *consts, h_end, rev)
        rg_out[0].append(hl)
        rg_out[1].append(hc)
    group = MXU_TILE // p['ml_w_q'].shape[-1]
    wq, wk, wv = (_dense_blocks(p[n], group) for n in ('ml_w_q', 'ml_w_k', 'ml_w_v'))
    wvt = jnp.swapaxes(wv, 1, 2)
    wg_all, bg_all = p['ml_w_gate'], p['ml_b_gate']
    nh = ML_HEADS
    wg = jnp.concatenate([wg_all[0][:, :nh], wg_all[1][:, :nh], wg_all[0][:, nh:], wg_all[1][:, nh:]], axis=1)
    bg = jnp.concatenate([bg_all[0][:nh], bg_all[1][:nh], bg_all[0][nh:], bg_all[1][nh:]])
    tril = jnp.tril(jnp.ones((ML_CHUNK, ML_CHUNK), BF16))
    prep_consts = (p['ml_conv_w'], vec(p['ml_conv_b']), wq, wk, wv, wvt, wg.astype(BF16), wg.T.astype(BF16),
                   bg.reshape(1, -1), bg.reshape(-1, 1), tril, tril.T)
    preps = [mlstm_prep(s[2], *prep_consts, dh ** -0.5) for s in streams]
    ml_out = [[], []]
    for rev in (False, True):
        state0 = (jnp.zeros((bsz, nh, dh, dh), F32), jnp.zeros((bsz, nh, 1, dh), F32),
                  jnp.zeros((bsz, nh, 1, 1), F32))
        seq = lambda pr: pr[:4] + pr[5:]
        mc, st = mlstm_direction(*seq(preps[1]), state0, rev)
        ml, _ = mlstm_direction(*seq(preps[0]), st, rev)
        ml_out[0].append(ml)
        ml_out[1].append(mc)
    w_out = p['w_out'].astype(BF16)
    outs = []
    for side, (x_in, mod) in enumerate(((x_lat, mod_l), (x_ctx, mod_c))):
        rg_x, rg_gate, ml_x, ml_z = streams[side]
        outs.append(even_out(x_in, mod, ln, (rg_out[side][0], rg_out[side][1], rg_gate, ml_out[side][0],
                                             ml_out[side][1], preps[side][4], ml_z),
                             vec(p['ml_norm_g']), vec(p['ml_skip']), w_out[:width], w_out[width:], alpha))
    return outs


def kernel(x, c, ctx, c_ctx, w_ada, b_ada, ln_g, ln_b, ffn_w_gate, ffn_w_up, ffn_w_down, ev_w_in, ev_w_out, rg_conv_w, rg_conv_b, rg_w_a, rg_b_a, rg_w_x, rg_b_x, rg_lambda, ml_conv_w, ml_conv_b, ml_w_q, ml_w_k, ml_w_v, ml_w_gate, ml_b_gate, ml_norm_g, ml_skip, od_w_in, od_w_out, cv_conv_w, cv_conv_b, cv_ln_g, cv_ln_b, da_lambda, da_subln_g):
    depth = w_ada.shape[0]
    d = x.shape[-1]
    alpha = (2 * depth) ** 0.25
    bsz, t_len = x.shape[:2]
    ctx_len = ctx.shape[1]
    rope = _rope_tables(t_len)
    x_lat, x_ctx = x, ctx
    cond_lat = jax.nn.silu(c)
    cond_ctx = jax.nn.silu(c_ctx)
    cw = cv_conv_w.shape[-1]
    dw = da_subln_g.shape[-1] * DA_HEADS
    for layer in range(depth):
        need_ctx = layer < depth - 1
        mod_l = (cond_lat @ w_ada[layer] + b_ada[layer]).reshape(bsz, N_SUB, 3, d)
        mod_c = (cond_ctx @ w_ada[layer] + b_ada[layer]).reshape(1, N_SUB, 3, d)
        ln = jnp.stack([ln_g[layer], ln_b[layer]], axis=1)
        ffn1 = _prep_ffn(ffn_w_gate[layer, 0], ffn_w_up[layer, 0], ffn_w_down[layer, 0])
        ffn2 = _prep_ffn(ffn_w_gate[layer, 1], ffn_w_up[layer, 1], ffn_w_down[layer, 1])
        x_lat = ffn_sublayer(x_lat, mod_l[:, 0], *ffn1, ln[0], alpha)
        x_ctx = ffn_sublayer(x_ctx, mod_c[:, 0], *ffn1, ln[0], alpha)
        i = layer // 2
        if layer % 2 == 0:
            p = dict(w_in=ev_w_in[i], w_out=ev_w_out[i], rg_conv_w=rg_conv_w[i], rg_conv_b=rg_conv_b[i],
                     rg_w_a=rg_w_a[i], rg_b_a=rg_b_a[i], rg_w_x=rg_w_x[i], rg_b_x=rg_b_x[i], rg_lambda=rg_lambda[i],
                     ml_conv_w=ml_conv_w[i], ml_conv_b=ml_conv_b[i], ml_w_q=ml_w_q[i], ml_w_k=ml_w_k[i],
                     ml_w_v=ml_w_v[i], ml_w_gate=ml_w_gate[i], ml_b_gate=ml_b_gate[i], ml_norm_g=ml_norm_g[i],
                     ml_skip=ml_skip[i])
            x_lat, x_ctx_new = _even_mixer(x_lat, x_ctx, mod_l[:, 1], mod_c[:, 1], ln[1], alpha, p)
        else:
            w_in = od_w_in[i].astype(BF16)
            lam_init = 0.8 - 0.6 * math.exp(-0.3 * layer)
            lamf = da_lambda[i].astype(F32)
            lam = (jnp.exp(jnp.sum(lamf[0] * lamf[1])) - jnp.exp(jnp.sum(lamf[2] * lamf[3])) + lam_init).reshape(1)
            subln = da_subln_g[i].reshape(1, -1)
            glu_l, q_l, k_l, v_l = inproj_odd(x_lat, mod_l[:, 1], w_in, rope, cw, dw)
            glu_c, q_c, k_c, v_c = inproj_odd(x_ctx, mod_c[:, 1], w_in, None, cw, dw)
            k_all = jnp.concatenate([k_c, k_l], axis=1)
            v_all = jnp.concatenate([v_c, v_l], axis=1)
            attn_l = diff_attention(q_l, k_all, v_all, lam, subln, 1.0 - lam_init, 512, ctx_len * 3)
            conv = (cv_conv_w[i], cv_conv_b[i], cv_ln_g[i], cv_ln_b[i])
            w_out = od_w_out[i].astype(BF16)
            w_parts = (w_out[:cw], w_out[cw:])
            x_lat = mixer_out(x_lat, mod_l[:, 1], ln[1], (conformer_conv(glu_l, *conv), attn_l), w_parts, alpha)
            if need_ctx:
                attn_c = diff_attention(q_c, k_c, v_c, lam, subln, 1.0 - lam_init, ctx_len, ctx_len)
                x_ctx_new = mixer_out(x_ctx, mod_c[:, 1], ln[1], (conformer_conv(glu_c, *conv), attn_c), w_parts,
                                      alpha)
        x_lat = ffn_sublayer(x_lat, mod_l[:, 2], *ffn2, ln[2], alpha)
        if need_ctx:
            x_ctx = ffn_sublayer(x_ctx_new, mod_c[:, 2], *ffn2, ln[2], alpha)
    return x_lat
```

```python
import functools
import math

import jax
import jax.numpy as jnp
from jax import lax
from jax.experimental import pallas as pl
from jax.experimental.pallas import tpu as pltpu

F32 = jnp.float32
BF16 = jnp.bfloat16

GRID_W = 64
N_SUB = 3
HALF_STEP = 0.5
LN_EPS = 1e-5
RG_C = 8.0
ML_HEADS = 4
DA_HEADS = 8
DA_HEAD_DIM = 64
DA_V_DIM = 2 * DA_HEAD_DIM
ROPE_BASE = 10000.0
LOG2_E = 1.4426950408889634

FFN_CHUNK = 256
MXU_TILE = 256
ML_CHUNK = 256
HALO = 8
CV_HALO = 16
CV_ROWS = 64
ATTN_TQ = 512
ATTN_KV = 512
ATTN_ONES = 16
VMEM_LIMIT = 56 * 1024 * 1024


def _params(*sem):
    return pltpu.CompilerParams(dimension_semantics=sem, vmem_limit_bytes=VMEM_LIMIT)


def _row_tile(t, want):
    tm = min(t, want)
    assert t % tm == 0
    return tm


def _layer_norm(z, g, b):
    mu = jnp.mean(z, axis=-1, keepdims=True)
    zc = z - mu
    var = jnp.mean(zc * zc, axis=-1, keepdims=True)
    return zc * lax.rsqrt(var + LN_EPS) * g + b


def _softplus(z):
    return jnp.maximum(z, 0.0) + jnp.log1p(jnp.exp(-jnp.abs(z)))


def _dot(a, b):
    return jnp.dot(a, b, preferred_element_type=F32)


def _dot_nt(a, b):
    return lax.dot_general(a, b, (((1,), (1,)), ((), ())), preferred_element_type=F32)


def _mod_spec(mod):
    per_sample = mod.shape[0] > 1
    return pl.BlockSpec((1,) + mod.shape[1:], lambda b, i: (b if per_sample else 0, 0, 0))


def _const_spec(shape):
    return pl.BlockSpec(shape, lambda b, i: (0,) * len(shape))


def _halo_specs(t, tm, width, halo, time_index):
    per_tile, n_halo = tm // halo, t // halo
    cur = pl.BlockSpec((1, tm, width), lambda b, i: (b, time_index(i), 0))
    prev = pl.BlockSpec((1, halo, width), lambda b, i: (b, jnp.maximum(time_index(i) * per_tile - 1, 0), 0))
    nxt = pl.BlockSpec((1, halo, width),
                       lambda b, i: (b, jnp.minimum((time_index(i) + 1) * per_tile, n_halo - 1), 0))
    return [cur, prev, nxt]


def _fill_padded(xe_sc, cur, prev_ref, next_ref, ti, n_tiles, halo):
    tm = cur.shape[0]
    xe_sc[0:halo] = prev_ref[0] * (ti > 0).astype(F32)
    xe_sc[halo:halo + tm] = cur
    xe_sc[halo + tm:2 * halo + tm] = next_ref[0] * (ti < n_tiles - 1).astype(F32)


def _short_conv(xe_sc, w_ref, b_ref, tm):
    y = b_ref[...] + w_ref[0:1] * xe_sc[pl.ds(HALO - 1, tm), :]
    for j in range(1, w_ref.shape[0]):
        y = y + w_ref[j:j + 1] * xe_sc[pl.ds(HALO - 1 + j, tm), :]
    return y


def _ffn_body(x_ref, mod_ref, wg_ref, wu_ref, wd_ref, ln_ref, o_ref, *, alpha):
    x = x_ref[0]
    mod = mod_ref[0]
    h = (x * (1.0 + mod[1:2]) + mod[0:1]).astype(BF16)
    y = jnp.zeros(x.shape, F32)
    for c in range(wg_ref.shape[0]):
        g = _dot(h, wg_ref[c])
        u = _dot(h, wu_ref[c])
        a = (g * jax.nn.sigmoid(g) * u).astype(BF16)
        y = y + _dot(a, wd_ref[c])
    z = alpha * x + (HALF_STEP * mod[2:3]) * y
    o_ref[0] = _layer_norm(z, ln_ref[0:1], ln_ref[1:2])


def ffn_sublayer(x, mod, wg, wu, wd, ln, alpha):
    bsz, t, d = x.shape
    tm = _row_tile(t, 512)
    return pl.pallas_call(
        functools.partial(_ffn_body, alpha=alpha),
        grid=(bsz, t // tm),
        in_specs=[pl.BlockSpec((1, tm, d), lambda b, i: (b, i, 0)), _mod_spec(mod),
                  _const_spec(wg.shape), _const_spec(wu.shape), _const_spec(wd.shape), _const_spec(ln.shape)],
        out_specs=pl.BlockSpec((1, tm, d), lambda b, i: (b, i, 0)),
        out_shape=jax.ShapeDtypeStruct(x.shape, F32),
        compiler_params=_params("parallel", "parallel"),
        name="ffn_sublayer",
    )(x, mod, wg, wu, wd, ln)


def _prep_ffn(wg, wu, wd):
    d, f = wg.shape
    n = f // FFN_CHUNK
    to_chunks = lambda w: w.astype(BF16).reshape(d, n, FFN_CHUNK).transpose(1, 0, 2)
    return to_chunks(wg), to_chunks(wu), wd.astype(BF16).reshape(n, FFN_CHUNK, d)


def _inproj_even_body(x_ref, mod_ref, w_ref, *o_refs):
    mod = mod_ref[0]
    h = (x_ref[0] * (1.0 + mod[1:2]) + mod[0:1]).astype(BF16)
    width = o_refs[0].shape[-1]
    for j, o_ref in enumerate(o_refs):
        o_ref[0] = _dot(h, w_ref[:, j * width:(j + 1) * width])


def inproj_even(x, mod, w_in):
    bsz, t, d = x.shape
    tm = _row_tile(t, 512)
    n_out = w_in.shape[1] // d
    row_spec = pl.BlockSpec((1, tm, d), lambda b, i: (b, i, 0))
    return pl.pallas_call(
        _inproj_even_body,
        grid=(bsz, t // tm),
        in_specs=[row_spec, _mod_spec(mod), _const_spec(w_in.shape)],
        out_specs=[row_spec] * n_out,
        out_shape=[jax.ShapeDtypeStruct(x.shape, F32)] * n_out,
        compiler_params=_params("parallel", "parallel"),
        name="inproj_even",
    )(x, mod, w_in)


def _rope128(a, cos, sin_lo, sin_hi):
    return a * cos + pltpu.roll(a, 112, 1) * sin_lo + pltpu.roll(a, 16, 1) * sin_hi


def _inproj_odd_body(x_ref, mod_ref, w_ref, wvt_ref, *rest, use_rope):
    if use_rope:
        cos_ref, slo_ref, shi_ref, glu_ref, q_ref, k_ref, vt_ref = rest
    else:
        glu_ref, q_ref, k_ref, vt_ref = rest
    mod = mod_ref[0]
    h = (x_ref[0] * (1.0 + mod[1:2]) + mod[0:1]).astype(BF16)
    cw = glu_ref.shape[-1]
    a = _dot(h, w_ref[:, :cw])
    gate = _dot(h, w_ref[:, cw:2 * cw])
    glu_ref[0] = a * jax.nn.sigmoid(gate)
    dw = q_ref.shape[-1]
    q = _dot(h, w_ref[:, 2 * cw:2 * cw + dw]) * (DA_HEAD_DIM ** -0.5 * LOG2_E)
    k = _dot(h, w_ref[:, 2 * cw + dw:2 * cw + 2 * dw])
    if use_rope:
        cos, slo, shi = cos_ref[...], slo_ref[...], shi_ref[...]
        for j in range(dw // 128):
            sl = slice(128 * j, 128 * (j + 1))
            q_ref[0, :, sl] = _rope128(q[:, sl], cos, slo, shi).astype(BF16)
            k_ref[0, :, sl] = _rope128(k[:, sl], cos, slo, shi).astype(BF16)
    else:
        q_ref[0] = q.astype(BF16)
        k_ref[0] = k.astype(BF16)
    vt_ref[0, 0] = _dot_nt(wvt_ref[...], h).astype(BF16)


def inproj_odd(x, mod, w_in, w_vt, rope, cw, dw):
    bsz, t, d = x.shape
    tm = _row_tile(t, ATTN_KV)
    row = lambda w: pl.BlockSpec((1, tm, w), lambda b, i: (b, i, 0))
    in_specs = [row(d), _mod_spec(mod), _const_spec(w_in.shape), _const_spec(w_vt.shape)]
    args = [x, mod, w_in, w_vt]
    if rope is not None:
        in_specs += [pl.BlockSpec((tm, 128), lambda b, i: (i, 0))] * 3
        args += list(rope)
    return pl.pallas_call(
        functools.partial(_inproj_odd_body, use_rope=rope is not None),
        grid=(bsz, t // tm),
        in_specs=in_specs,
        out_specs=[row(cw), row(dw), row(dw), pl.BlockSpec((1, 1, dw, tm), lambda b, i: (b, i, 0, 0))],
        out_shape=[jax.ShapeDtypeStruct((bsz, t, cw), F32)] + [jax.ShapeDtypeStruct((bsz, t, dw), BF16)] * 2
                  + [jax.ShapeDtypeStruct((bsz, t // tm, dw, tm), BF16)],
        compiler_params=_params("parallel", "parallel"),
        name="inproj_odd",
    )(*args)


def _rope_tables(t):
    pos = jnp.arange(t)
    row = (pos // GRID_W).astype(F32)
    col = (pos % GRID_W).astype(F32)
    axis_dim = DA_HEAD_DIM // 2
    inv_freq = ROPE_BASE ** (-jnp.arange(0, axis_dim, 2, dtype=F32) / axis_dim)
    ang_r = row[:, None] * inv_freq
    ang_c = col[:, None] * inv_freq
    zeros = jnp.zeros_like(ang_r)
    cos64 = jnp.concatenate([jnp.cos(ang_r)] * 2 + [jnp.cos(ang_c)] * 2, axis=-1)
    sin_lo64 = jnp.concatenate([-jnp.sin(ang_r), zeros, -jnp.sin(ang_c), zeros], axis=-1)
    sin_hi64 = jnp.concatenate([zeros, jnp.sin(ang_r), zeros, jnp.sin(ang_c)], axis=-1)
    return tuple(jnp.tile(a, (1, 2)) for a in (cos64, sin_lo64, sin_hi64))


def _attn_body(lam_ref, q_ref, kc_ref, vtc_ref, *rest, out_scale, has_lat):
    if has_lat:
        kl_ref, vtl_ref, g_ref, o_ref, m_sc, acc_sc, sa_sc, sb_sc = rest
    else:
        g_ref, o_ref, m_sc, acc_sc = rest
    q = q_ref[0]
    lane = lax.broadcasted_iota(jnp.int32, q.shape, 1)
    zero = jnp.zeros_like(q)
    q_maps = (jnp.where(lane < DA_HEAD_DIM, q, zero), jnp.where(lane >= DA_HEAD_DIM, q, zero))
    m_sc[...] = jnp.full(m_sc.shape, -jnp.inf, F32)
    acc_sc[...] = jnp.zeros(acc_sc.shape, F32)

    def scores(k):
        return [_dot_nt(k, q_maps[m]) for m in range(2)]

    def consume(st_maps, vt):
        vt_ext = jnp.concatenate([vt, jnp.ones((ATTN_ONES, vt.shape[1]), BF16)], axis=0)
        for m in range(2):
            st = st_maps[m]
            m_prev = m_sc[m]
            m_new = jnp.maximum(m_prev, jnp.max(st, axis=0, keepdims=True))
            p = jnp.exp2(st - m_new).astype(BF16)
            acc_sc[m] = jnp.exp2(m_prev - m_new) * acc_sc[m] + _dot(vt_ext, p)
            m_sc[m] = m_new

    consume(scores(kc_ref[0]), vtc_ref[0, 0])
    if has_lat:
        n_blocks = vtl_ref.shape[1]

        def k_block(j):
            return kl_ref[0, pl.ds(pl.multiple_of(j * ATTN_KV, ATTN_KV), ATTN_KV), :]

        def put(dst_sc, st_maps):
            dst_sc[0], dst_sc[1] = st_maps

        put(sa_sc, scores(k_block(0)))

        def step(jj, carry):
            j = 2 * jj
            put(sb_sc, scores(k_block(j + 1)))
            consume((sa_sc[0], sa_sc[1]), vtl_ref[0, j])
            put(sa_sc, scores(k_block(jnp.minimum(j + 2, n_blocks - 1))))
            consume((sb_sc[0], sb_sc[1]), vtl_ref[0, j + 1])
            return carry

        lax.fori_loop(0, n_blocks // 2, step, 0)

    o0 = acc_sc[0, :DA_V_DIM] / acc_sc[0, DA_V_DIM:DA_V_DIM + 1]
    o1 = acc_sc[1, :DA_V_DIM] / acc_sc[1, DA_V_DIM:DA_V_DIM + 1]
    o = o0 - lam_ref[0] * o1
    o = o * lax.rsqrt(jnp.mean(o * o, axis=0, keepdims=True) + LN_EPS) * (g_ref[...] * out_scale)
    o_ref[0] = o.T


def diff_attention(q, k_ctx, vt_ctx, k_lat, vt_lat, lam, subln_g, out_scale):
    bsz, t, dw = q.shape
    n_heads = dw // DA_V_DIM
    tq = _row_tile(t, ATTN_TQ)
    has_lat = k_lat is not None
    kv_specs = [pl.BlockSpec((1, k_ctx.shape[1], DA_V_DIM), lambda b, h, i: (b, 0, h)),
                pl.BlockSpec((1, 1, DA_V_DIM, vt_ctx.shape[-1]), lambda b, h, i: (b, 0, h, 0))]
    kv_args = [k_ctx, vt_ctx]
    if has_lat:
        assert vt_lat.shape[-1] == ATTN_KV
        kv_specs += [pl.BlockSpec((1, k_lat.shape[1], DA_V_DIM), lambda b, h, i: (b, 0, h)),
                     pl.BlockSpec((1, vt_lat.shape[1], DA_V_DIM, ATTN_KV), lambda b, h, i: (b, 0, h, 0))]
        kv_args += [k_lat, vt_lat]
        assert vt_lat.shape[1] % 2 == 0
    scratch = [pltpu.VMEM((2, 1, tq), F32), pltpu.VMEM((2, DA_V_DIM + ATTN_ONES, tq), F32)]
    if has_lat:
        scratch += [pltpu.VMEM((2, ATTN_KV, tq), F32)] * 2
    return pl.pallas_call(
        functools.partial(_attn_body, out_scale=out_scale, has_lat=has_lat),
        grid=(bsz, n_heads, t // tq),
        in_specs=[pl.BlockSpec(memory_space=pltpu.SMEM),
                  pl.BlockSpec((1, tq, DA_V_DIM), lambda b, h, i: (b, i, h))] + kv_specs
                 + [pl.BlockSpec((DA_V_DIM, 1), lambda b, h, i: (0, 0))],
        out_specs=pl.BlockSpec((1, tq, DA_V_DIM), lambda b, h, i: (b, i, h)),
        out_shape=jax.ShapeDtypeStruct((bsz, t, dw), F32),
        scratch_shapes=scratch,
        compiler_params=_params("parallel", "parallel", "parallel"),
        name="diff_attention",
    )(lam, q, *kv_args, subln_g)


def _cvconv_body(x_ref, prev_ref, next_ref, w_ref, b_ref, g_ref, beta_ref, o_ref, xe_sc):
    i, n = pl.program_id(1), pl.num_programs(1)
    tm = x_ref.shape[1]
    _fill_padded(xe_sc, x_ref[0], prev_ref, next_ref, i, n, CV_HALO)
    n_taps = w_ref.shape[0]
    first = CV_HALO - (n_taps - 1) // 2

    for base in range(0, tm, CV_ROWS):
        y = b_ref[...] + w_ref[0:1] * xe_sc[base + first:base + first + CV_ROWS, :]
        for j in range(1, n_taps):
            y = y + w_ref[j:j + 1] * xe_sc[base + first + j:base + first + j + CV_ROWS, :]
        y = _layer_norm(y, g_ref[...], beta_ref[...])
        o_ref[0, base:base + CV_ROWS, :] = y * jax.nn.sigmoid(y)


def conformer_conv(glu, conv_w, conv_b, ln_g, ln_b):
    bsz, t, cw = glu.shape
    tm = _row_tile(t, 512)
    vec = lambda a: a.reshape(1, cw)
    return pl.pallas_call(
        _cvconv_body,
        grid=(bsz, t // tm),
        in_specs=_halo_specs(t, tm, cw, CV_HALO, lambda i: i)
                 + [_const_spec(conv_w.shape)] + [_const_spec((1, cw))] * 3,
        out_specs=pl.BlockSpec((1, tm, cw), lambda b, i: (b, i, 0)),
        out_shape=jax.ShapeDtypeStruct(glu.shape, F32),
        scratch_shapes=[pltpu.VMEM((tm + 2 * CV_HALO, cw), F32)],
        compiler_params=_params("parallel", "parallel"),
        name="conformer_conv",
    )(glu, glu, glu, conv_w, vec(conv_b), vec(ln_g), vec(ln_b))


def _rglru_body(x_ref, prev_ref, next_ref, cw_ref, cb_ref, wa_ref, ba_ref, wx_ref, bx_ref, lam_ref, h0_ref,
                h_ref, hlast_ref, xe_sc, a_sc, b_sc, state_sc, *, reverse):
    i, n = pl.program_id(1), pl.num_programs(1)
    ti = n - 1 - i if reverse else i
    tm = x_ref.shape[1]

    @pl.when(i == 0)
    def _():
        state_sc[...] = h0_ref[0]

    _fill_padded(xe_sc, x_ref[0], prev_ref, next_ref, ti, n, HALO)
    u = _short_conv(xe_sc, cw_ref, cb_ref, tm)
    ub = u.astype(BF16)
    for c in range(wa_ref.shape[0]):
        sl = slice(MXU_TILE * c, MXU_TILE * (c + 1))
        r = jax.nn.sigmoid(_dot(ub[:, sl], wa_ref[c]) + ba_ref[:, sl])
        gate_in = jax.nn.sigmoid(_dot(ub[:, sl], wx_ref[c]) + bx_ref[:, sl])
        log_a = (-RG_C) * r * _softplus(-lam_ref[:, sl])
        a = jnp.exp(log_a)
        a_sc[:, sl] = a
        b_sc[:, sl] = jnp.sqrt(-jnp.tanh(log_a) * (a * a + 1.0)) * gate_in * u[:, sl]

    def step(s, h):
        row = tm - 1 - s if reverse else s
        h = a_sc[pl.ds(row, 1), :] * h + b_sc[pl.ds(row, 1), :]
        h_ref[0, pl.ds(row, 1), :] = h
        return h

    h = lax.fori_loop(0, tm, step, state_sc[...], unroll=8)
    state_sc[...] = h

    @pl.when(i == n - 1)
    def _():
        hlast_ref[0] = h


def rglru_direction(rg_x, conv_w, conv_b, w_a, b_a, w_x, b_x, lam, h0, reverse):
    bsz, t, width = rg_x.shape
    tm = _row_tile(t, 512)
    n = t // tm
    tidx = (lambda i: n - 1 - i) if reverse else (lambda i: i)
    vec = _const_spec((1, width))
    state_spec = pl.BlockSpec((1, 1, width), lambda b, i: (b, 0, 0))
    return pl.pallas_call(
        functools.partial(_rglru_body, reverse=reverse),
        grid=(bsz, n),
        in_specs=_halo_specs(t, tm, width, HALO, tidx)
                 + [_const_spec(conv_w.shape), vec, _const_spec(w_a.shape), vec, _const_spec(w_x.shape), vec, vec,
                    state_spec],
        out_specs=[pl.BlockSpec((1, tm, width), lambda b, i: (b, tidx(i), 0)), state_spec],
        out_shape=[jax.ShapeDtypeStruct(rg_x.shape, F32), jax.ShapeDtypeStruct((bsz, 1, width), F32)],
        scratch_shapes=[pltpu.VMEM((tm + 2 * HALO, width), F32), pltpu.VMEM((tm, width), F32),
                        pltpu.VMEM((tm, width), F32), pltpu.VMEM((1, width), F32)],
        compiler_params=_params("parallel", "arbitrary"),
        name="rglru_rev" if reverse else "rglru_fwd",
    )(rg_x, rg_x, rg_x, conv_w, conv_b, w_a, b_a, w_x, b_x, lam, h0)


def _split3(a):
    hi = a.astype(BF16)
    rest = a - hi.astype(F32)
    mid = rest.astype(BF16)
    return hi, mid, (rest - mid.astype(F32)).astype(BF16)


def _mlprep_body(x_ref, prev_ref, next_ref, cw_ref, cb_ref, wq_ref, wk_ref, wv_ref, wvt_ref, wg_ref, wgt_ref,
                 bg_ref, bgt_ref, tril_ref, triu_ref,
                 q_ref, k_ref, v_ref, vt_ref, mlc_ref, gcol_ref, grow_ref, xe_sc, *, k_scale):
    i, n = pl.program_id(1), pl.num_programs(1)
    tm = x_ref.shape[1]
    x = x_ref[0]
    _fill_padded(xe_sc, x, prev_ref, next_ref, i, n, HALO)
    conv = _short_conv(xe_sc, cw_ref, cb_ref, tm)
    ml_c = conv * jax.nn.sigmoid(conv)
    mlc_ref[0] = ml_c
    cb16, xb16 = ml_c.astype(BF16), x.astype(BF16)
    width = x.shape[1]
    n_gate = wg_ref.shape[1]
    pre = jnp.zeros((tm, n_gate), F32) + bg_ref[...]
    pre_t = jnp.zeros((n_gate, tm), F32) + bgt_ref[...]
    for c in range(wq_ref.shape[0]):
        sl = slice(MXU_TILE * c, MXU_TILE * (c + 1))
        q = _dot(cb16[:, sl], wq_ref[c]).astype(BF16)
        k = _dot(cb16[:, sl], wk_ref[c]).astype(BF16)
        v = _dot(xb16[:, sl], wv_ref[c]).astype(BF16)
        q_ref[0, :, sl] = q
        k_ref[0, :, sl] = k * k_scale
        v_ref[0, :, sl] = v
        vt_ref[0, sl, :] = _dot_nt(wvt_ref[c], xb16[:, sl]).astype(BF16)
        for part, val in enumerate((q, k, v)):
            rows = slice(part * width + MXU_TILE * c, part * width + MXU_TILE * (c + 1))
            pre = pre + _dot(val, wg_ref[rows, :])
            pre_t = pre_t + _dot_nt(wgt_ref[:, rows], val)
    half = n_gate // 2
    col = lax.broadcasted_iota(jnp.int32, (ML_CHUNK, n_gate), 1)
    row = lax.broadcasted_iota(jnp.int32, (n_gate, ML_CHUNK), 0)
    for c in range(tm // ML_CHUNK):
        ch = slice(ML_CHUNK * c, ML_CHUNK * (c + 1))
        pre_c, pre_tc = pre[ch], pre_t[:, ch]
        logf_c = jnp.where(col >= half, -_softplus(-pre_c), 0.0)
        logf_tc = jnp.where(row >= half, -_softplus(-pre_tc), 0.0)
        pieces, pieces_t = _split3(logf_c), _split3(logf_tc)
        prefix = sum(_dot(tril_ref[...], p) for p in pieces)
        suffix = sum(_dot(triu_ref[...], p) for p in pieces)
        prefix_t = sum(_dot(p, triu_ref[...]) for p in pieces_t)
        suffix_t = sum(_dot(p, tril_ref[...]) for p in pieces_t)
        gcol_ref[0, ch, :] = jnp.where(col < half, pre_c, jnp.where(col < half + half // 2, prefix, suffix))
        grow_ref[0, :, ch] = jnp.where(row < half, pre_tc, jnp.where(row < half + half // 2, prefix_t, suffix_t))


def mlstm_prep(ml_x, conv_w, conv_b, wq, wk, wv, wvt, wg, wgt, bg, bgt, tril, triu, k_scale):
    bsz, t, width = ml_x.shape
    tm = _row_tile(t, 512)
    n_gate = wg.shape[1]
    row = pl.BlockSpec((1, tm, width), lambda b, i: (b, i, 0))
    consts = (conv_w, conv_b, wq, wk, wv, wvt, wg, wgt, bg, bgt, tril, triu)
    return pl.pallas_call(
        functools.partial(_mlprep_body, k_scale=k_scale),
        grid=(bsz, t // tm),
        in_specs=_halo_specs(t, tm, width, HALO, lambda i: i) + [_const_spec(a.shape) for a in consts],
        out_specs=[row, row, row, pl.BlockSpec((1, width, tm), lambda b, i: (b, 0, i)), row,
                   pl.BlockSpec((1, tm, n_gate), lambda b, i: (b, i, 0)),
                   pl.BlockSpec((1, n_gate, tm), lambda b, i: (b, 0, i))],
        out_shape=[jax.ShapeDtypeStruct(ml_x.shape, BF16)] * 3
                  + [jax.ShapeDtypeStruct((bsz, width, t), BF16), jax.ShapeDtypeStruct(ml_x.shape, F32),
                     jax.ShapeDtypeStruct((bsz, t, n_gate), F32), jax.ShapeDtypeStruct((bsz, n_gate, t), F32)],
        scratch_shapes=[pltpu.VMEM((tm + 2 * HALO, width), F32)],
        compiler_params=_params("parallel", "parallel"),
        name="mlstm_prep",
    )(ml_x, ml_x, ml_x, *consts)


def _mlstm_body(q_ref, k_ref, v_ref, vt_ref, gcol_ref, grow_ref, c0_ref, n0_ref, m0_ref,
                h_ref, cf_ref, nf_ref, mf_ref, c_sc, n_sc, m_sc, *, reverse):
    i, n = pl.program_id(1), pl.num_programs(1)
    length = q_ref.shape[1]
    n_heads = c_sc.shape[0]
    dh = c_sc.shape[1]

    @pl.when(i == 0)
    def _():
        c_sc[...] = c0_ref[0]
        n_sc[...] = n0_ref[0]
        m_sc[...] = m0_ref[0]

    direction = 1 if reverse else 0
    t_idx = lax.broadcasted_iota(jnp.int32, (length, length), 0)
    s_idx = lax.broadcasted_iota(jnp.int32, (length, length), 1)
    visible = (s_idx >= t_idx) if reverse else (s_idx <= t_idx)
    for h in range(n_heads):
        sl = slice(dh * h, dh * (h + 1))
        ci = n_heads * direction + h
        cb = 2 * n_heads + ci
        qh, kh, vh, vth = q_ref[0, :, sl], k_ref[0, :, sl], v_ref[0, :, sl], vt_ref[0, sl, :]
        ig_row, b_row = grow_ref[0, ci:ci + 1, :], grow_ref[0, cb:cb + 1, :]
        b_col = gcol_ref[0, :, cb:cb + 1]
        g = b_col[0:1] if reverse else b_col[length - 1:length]
        m_st = m_sc[h]
        log_w = jnp.where(visible, b_col - b_row + ig_row, -jnp.inf)
        log_inter = b_col + m_st
        m_t = jnp.maximum(log_inter, jnp.max(log_w, axis=-1, keepdims=True))
        w = jnp.exp(log_w - m_t) * _dot_nt(qh, kh)
        s_inter = jnp.exp(log_inter - m_t)
        num = _dot(w.astype(BF16), vh) + s_inter * _dot_nt(qh, c_sc[h].astype(BF16))
        den = jnp.sum(w, axis=-1, keepdims=True) \
            + s_inter * jnp.sum(qh.astype(F32) * n_sc[h], axis=-1, keepdims=True)
        h_ref[0, :, sl] = num / jnp.maximum(jnp.abs(den), jnp.exp(-m_t))
        log_end = g - b_row + ig_row
        m_new = jnp.maximum(g + m_st, jnp.max(log_end, axis=-1, keepdims=True))
        e_row = jnp.exp(log_end - m_new)
        decay = jnp.exp(g + m_st - m_new)
        c_sc[h] = decay * c_sc[h] + _dot((vth * e_row).astype(BF16), kh)
        e8 = jnp.broadcast_to(e_row, (8, length)).astype(BF16)
        n_sc[h] = decay * n_sc[h] + _dot(e8, kh)[0:1]
        m_sc[h] = m_new

    @pl.when(i == n - 1)
    def _():
        cf_ref[0] = c_sc[...]
        nf_ref[0] = n_sc[...]
        mf_ref[0] = m_sc[...]


def mlstm_direction(q, k, v, vt, gcol, grow, state, reverse):
    bsz, t, width = q.shape
    n = t // ML_CHUNK
    n_gate = gcol.shape[-1]
    dh = width // ML_HEADS
    tidx = (lambda i: n - 1 - i) if reverse else (lambda i: i)
    row = pl.BlockSpec((1, ML_CHUNK, width), lambda b, i: (b, tidx(i), 0))
    state_specs = [pl.BlockSpec((1,) + s.shape[1:], lambda b, i: (b, 0, 0, 0)) for s in state]
    outs = pl.pallas_call(
        functools.partial(_mlstm_body, reverse=reverse),
        grid=(bsz, n),
        in_specs=[row, row, row, pl.BlockSpec((1, width, ML_CHUNK), lambda b, i: (b, 0, tidx(i))),
                  pl.BlockSpec((1, ML_CHUNK, n_gate), lambda b, i: (b, tidx(i), 0)),
                  pl.BlockSpec((1, n_gate, ML_CHUNK), lambda b, i: (b, 0, tidx(i)))] + state_specs,
        out_specs=[row] + state_specs,
        out_shape=[jax.ShapeDtypeStruct(q.shape, F32)] + [jax.ShapeDtypeStruct(s.shape, F32) for s in state],
        scratch_shapes=[pltpu.VMEM((ML_HEADS, dh, dh), F32), pltpu.VMEM((ML_HEADS, 1, dh), F32),
                        pltpu.VMEM((ML_HEADS, 1, 1), F32)],
        compiler_params=_params("parallel", "arbitrary"),
        name="mlstm_rev" if reverse else "mlstm_fwd",
    )(q, k, v, vt, gcol, grow, *state)
    return outs[0], tuple(outs[1:])


def _mixout_body(*refs, n_parts, alpha):
    x_ref, mod_ref, ln_ref = refs[:3]
    part_refs, w_refs, o_ref = refs[3:3 + n_parts], refs[3 + n_parts:3 + 2 * n_parts], refs[-1]
    x = x_ref[0]
    y = jnp.zeros(x.shape, F32)
    for p_ref, w_ref in zip(part_refs, w_refs):
        y = y + _dot(p_ref[0].astype(BF16), w_ref[...])
    z = alpha * x + mod_ref[0][2:3] * y
    o_ref[0] = _layer_norm(z, ln_ref[0:1], ln_ref[1:2])


def mixer_out(x, mod, ln, parts, weights, alpha):
    bsz, t, d = x.shape
    tm = _row_tile(t, 512)
    row = lambda w: pl.BlockSpec((1, tm, w), lambda b, i: (b, i, 0))
    return pl.pallas_call(
        functools.partial(_mixout_body, n_parts=len(parts), alpha=alpha),
        grid=(bsz, t // tm),
        in_specs=[row(d), _mod_spec(mod), _const_spec(ln.shape)] + [row(p.shape[-1]) for p in parts]
                 + [_const_spec(w.shape) for w in weights],
        out_specs=row(d),
        out_shape=jax.ShapeDtypeStruct(x.shape, F32),
        compiler_params=_params("parallel", "parallel"),
        name="mixer_out",
    )(x, mod, ln, *parts, *weights)


def _evenout_body(x_ref, mod_ref, ln_ref, rgf_ref, rgr_ref, gate_ref, mlf_ref, mlr_ref, mlc_ref, mlz_ref,
                  ng_ref, skip_ref, wrg_ref, wml_ref, o_ref, *, alpha):
    x = x_ref[0]
    y_rg = (rgf_ref[0] + rgr_ref[0]) * jax.nn.gelu(gate_ref[0], approximate=True)
    y = _dot(y_rg.astype(BF16), wrg_ref[...])
    h_ml = mlf_ref[0] + mlr_ref[0]
    z_gate = mlz_ref[0]
    dh = h_ml.shape[1] // ML_HEADS
    for h in range(ML_HEADS):
        sl = slice(dh * h, dh * (h + 1))
        hh = h_ml[:, sl]
        mu = jnp.mean(hh, axis=-1, keepdims=True)
        hc = hh - mu
        var = jnp.mean(hc * hc, axis=-1, keepdims=True)
        hn = hc * lax.rsqrt(var + LN_EPS) * ng_ref[:, sl]
        zg = z_gate[:, sl]
        y_ml = (hn + skip_ref[:, sl] * mlc_ref[0, :, sl]) * (zg * jax.nn.sigmoid(zg))
        y = y + _dot(y_ml.astype(BF16), wml_ref[sl, :])
    z = alpha * x + mod_ref[0][2:3] * y
    o_ref[0] = _layer_norm(z, ln_ref[0:1], ln_ref[1:2])


def even_out(x, mod, ln, streams, norm_g, skip, w_rg, w_ml, alpha):
    bsz, t, d = x.shape
    tm = _row_tile(t, 256)
    row = pl.BlockSpec((1, tm, d), lambda b, i: (b, i, 0))
    vec = _const_spec((1, d))
    return pl.pallas_call(
        functools.partial(_evenout_body, alpha=alpha),
        grid=(bsz, t // tm),
        in_specs=[row, _mod_spec(mod), _const_spec(ln.shape)] + [row] * len(streams)
                 + [vec, vec, _const_spec(w_rg.shape), _const_spec(w_ml.shape)],
        out_specs=row,
        out_shape=jax.ShapeDtypeStruct(x.shape, F32),
        compiler_params=_params("parallel", "parallel"),
        name="even_out",
    )(x, mod, ln, *streams, norm_g, skip, w_rg, w_ml)


def _dense_blocks(w, group):
    g, di, do = w.shape
    wb = w.reshape(g // group, group, di, do)
    eye = jnp.eye(group, dtype=w.dtype)
    return jnp.einsum('cgio,gh->cgiho', wb, eye).reshape(g // group, group * di, group * do).astype(BF16)


def _even_mixer(x_lat, x_ctx, mod_l, mod_c, ln, alpha, p):
    d = x_lat.shape[-1]
    bsz = x_lat.shape[0]
    w_in = p['w_in'].astype(BF16)
    streams = [inproj_even(x_lat, mod_l, w_in), inproj_even(x_ctx, mod_c, w_in)]
    width = streams[0][0].shape[-1]
    dh = width // ML_HEADS
    vec = lambda a: a.reshape(1, -1)
    rg_out = [[], []]
    for dr, rev in enumerate((False, True)):
        consts = (p['rg_conv_w'], vec(p['rg_conv_b']), _dense_blocks(p['rg_w_a'][dr], MXU_TILE // p['rg_w_a'].shape[-1]),
                  vec(p['rg_b_a'][dr]), _dense_blocks(p['rg_w_x'][dr], MXU_TILE // p['rg_w_x'].shape[-1]),
                  vec(p['rg_b_x'][dr]), vec(p['rg_lambda'][dr]))
        hc, h_end = rglru_direction(streams[1][0], *consts, jnp.zeros((bsz, 1, width), F32), rev)
        hl, _ = rglru_direction(streams[0][0], *consts, h_end, rev)
        rg_out[0].append(hl)
        rg_out[1].append(hc)
    group = MXU_TILE // p['ml_w_q'].shape[-1]
    wq, wk, wv = (_dense_blocks(p[n], group) for n in ('ml_w_q', 'ml_w_k', 'ml_w_v'))
    wvt = jnp.swapaxes(wv, 1, 2)
    wg_all, bg_all = p['ml_w_gate'], p['ml_b_gate']
    nh = ML_HEADS
    wg = jnp.concatenate([wg_all[0][:, :nh], wg_all[1][:, :nh], wg_all[0][:, nh:], wg_all[1][:, nh:]], axis=1)
    bg = jnp.concatenate([bg_all[0][:nh], bg_all[1][:nh], bg_all[0][nh:], bg_all[1][nh:]])
    tril = jnp.tril(jnp.ones((ML_CHUNK, ML_CHUNK), BF16))
    prep_consts = (p['ml_conv_w'], vec(p['ml_conv_b']), wq, wk, wv, wvt, wg.astype(BF16), wg.T.astype(BF16),
                   bg.reshape(1, -1), bg.reshape(-1, 1), tril, tril.T)
    preps = [mlstm_prep(s[2], *prep_consts, dh ** -0.5) for s in streams]
    ml_out = [[], []]
    for rev in (False, True):
        state0 = (jnp.zeros((bsz, nh, dh, dh), F32), jnp.zeros((bsz, nh, 1, dh), F32),
                  jnp.zeros((bsz, nh, 1, 1), F32))
        seq = lambda pr: pr[:4] + pr[5:]
        mc, st = mlstm_direction(*seq(preps[1]), state0, rev)
        ml, _ = mlstm_direction(*seq(preps[0]), st, rev)
        ml_out[0].append(ml)
        ml_out[1].append(mc)
    w_out = p['w_out'].astype(BF16)
    outs = []
    for side, (x_in, mod) in enumerate(((x_lat, mod_l), (x_ctx, mod_c))):
        rg_x, rg_gate, ml_x, ml_z = streams[side]
        outs.append(even_out(x_in, mod, ln, (rg_out[side][0], rg_out[side][1], rg_gate, ml_out[side][0],
                                             ml_out[side][1], preps[side][4], ml_z),
                             vec(p['ml_norm_g']), vec(p['ml_skip']), w_out[:width], w_out[width:], alpha))
    return outs


def kernel(x, c, ctx, c_ctx, w_ada, b_ada, ln_g, ln_b, ffn_w_gate, ffn_w_up, ffn_w_down, ev_w_in, ev_w_out, rg_conv_w, rg_conv_b, rg_w_a, rg_b_a, rg_w_x, rg_b_x, rg_lambda, ml_conv_w, ml_conv_b, ml_w_q, ml_w_k, ml_w_v, ml_w_gate, ml_b_gate, ml_norm_g, ml_skip, od_w_in, od_w_out, cv_conv_w, cv_conv_b, cv_ln_g, cv_ln_b, da_lambda, da_subln_g):
    depth = w_ada.shape[0]
    d = x.shape[-1]
    alpha = (2 * depth) ** 0.25
    bsz, t_len = x.shape[:2]
    ctx_len = ctx.shape[1]
    rope = _rope_tables(t_len)
    x_lat, x_ctx = x, ctx
    cond_lat = jax.nn.silu(c)
    cond_ctx = jax.nn.silu(c_ctx)
    cw = cv_conv_w.shape[-1]
    dw = da_subln_g.shape[-1] * DA_HEADS
    for layer in range(depth):
        need_ctx = layer < depth - 1
        mod_l = (cond_lat @ w_ada[layer] + b_ada[layer]).reshape(bsz, N_SUB, 3, d)
        mod_c = (cond_ctx @ w_ada[layer] + b_ada[layer]).reshape(1, N_SUB, 3, d)
        ln = jnp.stack([ln_g[layer], ln_b[layer]], axis=1)
        ffn1 = _prep_ffn(ffn_w_gate[layer, 0], ffn_w_up[layer, 0], ffn_w_down[layer, 0])
        ffn2 = _prep_ffn(ffn_w_gate[layer, 1], ffn_w_up[layer, 1], ffn_w_down[layer, 1])
        x_lat = ffn_sublayer(x_lat, mod_l[:, 0], *ffn1, ln[0], alpha)
        x_ctx = ffn_sublayer(x_ctx, mod_c[:, 0], *ffn1, ln[0], alpha)
        i = layer // 2
        if layer % 2 == 0:
            p = dict(w_in=ev_w_in[i], w_out=ev_w_out[i], rg_conv_w=rg_conv_w[i], rg_conv_b=rg_conv_b[i],
                     rg_w_a=rg_w_a[i], rg_b_a=rg_b_a[i], rg_w_x=rg_w_x[i], rg_b_x=rg_b_x[i], rg_lambda=rg_lambda[i],
                     ml_conv_w=ml_conv_w[i], ml_conv_b=ml_conv_b[i], ml_w_q=ml_w_q[i], ml_w_k=ml_w_k[i],
                     ml_w_v=ml_w_v[i], ml_w_gate=ml_w_gate[i], ml_b_gate=ml_b_gate[i], ml_norm_g=ml_norm_g[i],
                     ml_skip=ml_skip[i])
            x_lat, x_ctx_new = _even_mixer(x_lat, x_ctx, mod_l[:, 1], mod_c[:, 1], ln[1], alpha, p)
        else:
            w_in = od_w_in[i].astype(BF16)
            w_vt = w_in[:, 2 * cw + 2 * dw:].T
            lam_init = 0.8 - 0.6 * math.exp(-0.3 * layer)
            lamf = da_lambda[i].astype(F32)
            lam = (jnp.exp(jnp.sum(lamf[0] * lamf[1])) - jnp.exp(jnp.sum(lamf[2] * lamf[3])) + lam_init).reshape(1)
            subln = da_subln_g[i].reshape(-1, 1)
            glu_l, q_l, k_l, vt_l = inproj_odd(x_lat, mod_l[:, 1], w_in, w_vt, rope, cw, dw)
            glu_c, q_c, k_c, vt_c = inproj_odd(x_ctx, mod_c[:, 1], w_in, w_vt, None, cw, dw)
            attn_l = diff_attention(q_l, k_c, vt_c, k_l, vt_l, lam, subln, 1.0 - lam_init)
            conv = (cv_conv_w[i], cv_conv_b[i], cv_ln_g[i], cv_ln_b[i])
            w_out = od_w_out[i].astype(BF16)
            w_parts = (w_out[:cw], w_out[cw:])
            x_lat = mixer_out(x_lat, mod_l[:, 1], ln[1], (conformer_conv(glu_l, *conv), attn_l), w_parts, alpha)
            if need_ctx:
                attn_c = diff_attention(q_c, k_c, vt_c, None, None, lam, subln, 1.0 - lam_init)
                x_ctx_new = mixer_out(x_ctx, mod_c[:, 1], ln[1], (conformer_conv(glu_c, *conv), attn_c), w_parts,
                                      alpha)
        x_lat = ffn_sublayer(x_lat, mod_l[:, 2], *ffn2, ln[2], alpha)
        if need_ctx:
            x_ctx = ffn_sublayer(x_ctx_new, mod_c[:, 2], *ffn2, ln[2], alpha)
    return x_lat
```

```python
import functools
import math

import jax
import jax.numpy as jnp
from jax import lax
from jax.experimental import pallas as pl
from jax.experimental.pallas import tpu as pltpu

F32 = jnp.float32
BF16 = jnp.bfloat16

GRID_W = 64
N_SUB = 3
HALF_STEP = 0.5
LN_EPS = 1e-5
RG_C = 8.0
ML_HEADS = 4
DA_HEADS = 8
DA_HEAD_DIM = 64
DA_V_DIM = 2 * DA_HEAD_DIM
ROPE_BASE = 10000.0
LOG2_E = 1.4426950408889634

FFN_CHUNK = 256
MXU_TILE = 256
ML_CHUNK = 256
HALO = 8
CV_HALO = 16
CV_ROWS = 32
SUBLANES = 8
ATTN_TQ = 512
ATTN_KV = 512
ATTN_ONES = 16
VMEM_LIMIT = 56 * 1024 * 1024


def _params(*sem):
    return pltpu.CompilerParams(dimension_semantics=sem, vmem_limit_bytes=VMEM_LIMIT)


def _row_tile(t, want):
    tm = min(t, want)
    assert t % tm == 0
    return tm


def _layer_norm(z, g, b):
    mu = jnp.mean(z, axis=-1, keepdims=True)
    zc = z - mu
    var = jnp.mean(zc * zc, axis=-1, keepdims=True)
    return zc * lax.rsqrt(var + LN_EPS) * g + b


def _softplus(z):
    return jnp.maximum(z, 0.0) + jnp.log1p(jnp.exp(-jnp.abs(z)))


def _dot(a, b):
    return jnp.dot(a, b, preferred_element_type=F32)


def _dot_nt(a, b):
    return lax.dot_general(a, b, (((1,), (1,)), ((), ())), preferred_element_type=F32)


def _mod_spec(mod):
    per_sample = mod.shape[0] > 1
    return pl.BlockSpec((1,) + mod.shape[1:], lambda b, i: (b if per_sample else 0, 0, 0))


def _const_spec(shape):
    return pl.BlockSpec(shape, lambda b, i: (0,) * len(shape))


def _halo_specs(t, tm, width, halo, time_index):
    per_tile, n_halo = tm // halo, t // halo
    cur = pl.BlockSpec((1, tm, width), lambda b, i: (b, time_index(i), 0))
    prev = pl.BlockSpec((1, halo, width), lambda b, i: (b, jnp.maximum(time_index(i) * per_tile - 1, 0), 0))
    nxt = pl.BlockSpec((1, halo, width),
                       lambda b, i: (b, jnp.minimum((time_index(i) + 1) * per_tile, n_halo - 1), 0))
    return [cur, prev, nxt]


def _fill_padded(xe_sc, cur, prev_ref, next_ref, ti, n_tiles, halo):
    tm = cur.shape[0]
    xe_sc[0:halo] = prev_ref[0] * (ti > 0).astype(F32)
    xe_sc[halo:halo + tm] = cur
    xe_sc[halo + tm:2 * halo + tm] = next_ref[0] * (ti < n_tiles - 1).astype(F32)


def _short_conv(xe_sc, w_ref, b_ref, tm):
    y = b_ref[...] + w_ref[0:1] * xe_sc[pl.ds(HALO - 1, tm), :]
    for j in range(1, w_ref.shape[0]):
        y = y + w_ref[j:j + 1] * xe_sc[pl.ds(HALO - 1 + j, tm), :]
    return y


def _ffn_body(x_ref, mod_ref, wg_ref, wu_ref, wd_ref, ln_ref, o_ref, *, alpha):
    x = x_ref[0]
    mod = mod_ref[0]
    h = (x * (1.0 + mod[1:2]) + mod[0:1]).astype(BF16)
    y = jnp.zeros(x.shape, F32)
    for c in range(wg_ref.shape[0]):
        g = _dot(h, wg_ref[c])
        u = _dot(h, wu_ref[c])
        a = (g * jax.nn.sigmoid(g) * u).astype(BF16)
        y = y + _dot(a, wd_ref[c])
    z = alpha * x + (HALF_STEP * mod[2:3]) * y
    o_ref[0] = _layer_norm(z, ln_ref[0:1], ln_ref[1:2])


def ffn_sublayer(x, mod, wg, wu, wd, ln, alpha):
    bsz, t, d = x.shape
    tm = _row_tile(t, 512)
    return pl.pallas_call(
        functools.partial(_ffn_body, alpha=alpha),
        grid=(bsz, t // tm),
        in_specs=[pl.BlockSpec((1, tm, d), lambda b, i: (b, i, 0)), _mod_spec(mod),
                  _const_spec(wg.shape), _const_spec(wu.shape), _const_spec(wd.shape), _const_spec(ln.shape)],
        out_specs=pl.BlockSpec((1, tm, d), lambda b, i: (b, i, 0)),
        out_shape=jax.ShapeDtypeStruct(x.shape, F32),
        compiler_params=_params("parallel", "parallel"),
        name="ffn_sublayer",
    )(x, mod, wg, wu, wd, ln)


def _prep_ffn(wg, wu, wd):
    d, f = wg.shape
    n = f // FFN_CHUNK
    to_chunks = lambda w: w.astype(BF16).reshape(d, n, FFN_CHUNK).transpose(1, 0, 2)
    return to_chunks(wg), to_chunks(wu), wd.astype(BF16).reshape(n, FFN_CHUNK, d)


def _inproj_even_body(x_ref, mod_ref, w_ref, *o_refs):
    mod = mod_ref[0]
    h = (x_ref[0] * (1.0 + mod[1:2]) + mod[0:1]).astype(BF16)
    width = o_refs[0].shape[-1]
    for j, o_ref in enumerate(o_refs):
        o_ref[0] = _dot(h, w_ref[:, j * width:(j + 1) * width])


def inproj_even(x, mod, w_in):
    bsz, t, d = x.shape
    tm = _row_tile(t, 512)
    n_out = w_in.shape[1] // d
    row_spec = pl.BlockSpec((1, tm, d), lambda b, i: (b, i, 0))
    return pl.pallas_call(
        _inproj_even_body,
        grid=(bsz, t // tm),
        in_specs=[row_spec, _mod_spec(mod), _const_spec(w_in.shape)],
        out_specs=[row_spec] * n_out,
        out_shape=[jax.ShapeDtypeStruct(x.shape, F32)] * n_out,
        compiler_params=_params("parallel", "parallel"),
        name="inproj_even",
    )(x, mod, w_in)


def _rope128(a, cos, sin_lo, sin_hi):
    return a * cos + pltpu.roll(a, 112, 1) * sin_lo + pltpu.roll(a, 16, 1) * sin_hi


def _inproj_odd_body(x_ref, mod_ref, w_ref, wvt_ref, *rest, use_rope):
    if use_rope:
        cos_ref, slo_ref, shi_ref, glu_ref, q_ref, k_ref, vt_ref = rest
    else:
        glu_ref, q_ref, k_ref, vt_ref = rest
    mod = mod_ref[0]
    h = (x_ref[0] * (1.0 + mod[1:2]) + mod[0:1]).astype(BF16)
    cw = glu_ref.shape[-1]
    a = _dot(h, w_ref[:, :cw])
    gate = _dot(h, w_ref[:, cw:2 * cw])
    glu_ref[0] = a * jax.nn.sigmoid(gate)
    dw = q_ref.shape[-1]
    q = _dot(h, w_ref[:, 2 * cw:2 * cw + dw]) * (DA_HEAD_DIM ** -0.5 * LOG2_E)
    k = _dot(h, w_ref[:, 2 * cw + dw:2 * cw + 2 * dw])
    if use_rope:
        cos, slo, shi = cos_ref[...], slo_ref[...], shi_ref[...]
        for j in range(dw // 128):
            sl = slice(128 * j, 128 * (j + 1))
            q_ref[0, :, sl] = _rope128(q[:, sl], cos, slo, shi).astype(BF16)
            k_ref[0, :, sl] = _rope128(k[:, sl], cos, slo, shi).astype(BF16)
    else:
        q_ref[0] = q.astype(BF16)
        k_ref[0] = k.astype(BF16)
    vt_ref[0, 0] = _dot_nt(wvt_ref[...], h).astype(BF16)


def inproj_odd(x, mod, w_in, w_vt, rope, cw, dw):
    bsz, t, d = x.shape
    tm = _row_tile(t, ATTN_KV)
    row = lambda w: pl.BlockSpec((1, tm, w), lambda b, i: (b, i, 0))
    in_specs = [row(d), _mod_spec(mod), _const_spec(w_in.shape), _const_spec(w_vt.shape)]
    args = [x, mod, w_in, w_vt]
    if rope is not None:
        in_specs += [pl.BlockSpec((tm, 128), lambda b, i: (i, 0))] * 3
        args += list(rope)
    return pl.pallas_call(
        functools.partial(_inproj_odd_body, use_rope=rope is not None),
        grid=(bsz, t // tm),
        in_specs=in_specs,
        out_specs=[row(cw), row(dw), row(dw), pl.BlockSpec((1, 1, dw, tm), lambda b, i: (b, i, 0, 0))],
        out_shape=[jax.ShapeDtypeStruct((bsz, t, cw), F32)] + [jax.ShapeDtypeStruct((bsz, t, dw), BF16)] * 2
                  + [jax.ShapeDtypeStruct((bsz, t // tm, dw, tm), BF16)],
        compiler_params=_params("parallel", "parallel"),
        name="inproj_odd",
    )(*args)


def _rope_tables(t):
    pos = jnp.arange(t)
    row = (pos // GRID_W).astype(F32)
    col = (pos % GRID_W).astype(F32)
    axis_dim = DA_HEAD_DIM // 2
    inv_freq = ROPE_BASE ** (-jnp.arange(0, axis_dim, 2, dtype=F32) / axis_dim)
    ang_r = row[:, None] * inv_freq
    ang_c = col[:, None] * inv_freq
    zeros = jnp.zeros_like(ang_r)
    cos64 = jnp.concatenate([jnp.cos(ang_r)] * 2 + [jnp.cos(ang_c)] * 2, axis=-1)
    sin_lo64 = jnp.concatenate([-jnp.sin(ang_r), zeros, -jnp.sin(ang_c), zeros], axis=-1)
    sin_hi64 = jnp.concatenate([zeros, jnp.sin(ang_r), zeros, jnp.sin(ang_c)], axis=-1)
    return tuple(jnp.tile(a, (1, 2)) for a in (cos64, sin_lo64, sin_hi64))


def _attn_body(lam_ref, q_ref, kc_ref, vtc_ref, *rest, out_scale, has_lat):
    if has_lat:
        kl_ref, vtl_ref, g_ref, o_ref, m_sc, acc_sc, pa_sc, ca_sc, sa_sc, sb_sc, pb_sc, cb_sc = rest
    else:
        g_ref, o_ref, m_sc, acc_sc, pa_sc, ca_sc = rest
    q = q_ref[0]
    lane = lax.broadcasted_iota(jnp.int32, q.shape, 1)
    zero = jnp.zeros_like(q)
    q_maps = (jnp.where(lane < DA_HEAD_DIM, q, zero), jnp.where(lane >= DA_HEAD_DIM, q, zero))
    m_sc[...] = jnp.full(m_sc.shape, -jnp.inf, F32)
    acc_sc[...] = jnp.zeros(acc_sc.shape, F32)

    def scores(k):
        return [_dot_nt(k, q_maps[m]) for m in range(2)]

    def softmax(st_maps, p_sc, c_sc):
        tk = st_maps[0].shape[0]
        for m in range(2):
            m_prev = m_sc[m]
            m_new = jnp.maximum(m_prev, jnp.max(st_maps[m], axis=0, keepdims=True))
            p_sc[m, 0:tk] = jnp.exp2(st_maps[m] - m_new).astype(BF16)
            c_sc[m] = jnp.exp2(m_prev - m_new)
            m_sc[m] = m_new

    def accumulate(vt, p_sc, c_sc):
        tk = vt.shape[1]
        vt_ext = jnp.concatenate([vt, jnp.ones((ATTN_ONES, tk), BF16)], axis=0)
        for m in range(2):
            acc_sc[m] = c_sc[m] * acc_sc[m] + _dot(vt_ext, p_sc[m, 0:tk])

    softmax(scores(kc_ref[0]), pa_sc, ca_sc)
    accumulate(vtc_ref[0, 0], pa_sc, ca_sc)
    if has_lat:
        n_blocks = vtl_ref.shape[1]

        def k_block(j):
            return kl_ref[0, pl.ds(pl.multiple_of(j * ATTN_KV, ATTN_KV), ATTN_KV), :]

        def put(dst_sc, st_maps):
            dst_sc[0], dst_sc[1] = st_maps

        put(sa_sc, scores(k_block(0)))
        pb_sc[...] = jnp.zeros(pb_sc.shape, BF16)
        cb_sc[...] = jnp.ones(cb_sc.shape, F32)

        def step(jj, carry):
            j = 2 * jj
            put(sb_sc, scores(k_block(j + 1)))
            accumulate(vtl_ref[0, jnp.maximum(j - 1, 0)], pb_sc, cb_sc)
            softmax((sa_sc[0], sa_sc[1]), pa_sc, ca_sc)
            put(sa_sc, scores(k_block(jnp.minimum(j + 2, n_blocks - 1))))
            accumulate(vtl_ref[0, j], pa_sc, ca_sc)
            softmax((sb_sc[0], sb_sc[1]), pb_sc, cb_sc)
            return carry

        lax.fori_loop(0, n_blocks // 2, step, 0)
        accumulate(vtl_ref[0, n_blocks - 1], pb_sc, cb_sc)

    o0 = acc_sc[0, :DA_V_DIM] / acc_sc[0, DA_V_DIM:DA_V_DIM + 1]
    o1 = acc_sc[1, :DA_V_DIM] / acc_sc[1, DA_V_DIM:DA_V_DIM + 1]
    o = o0 - lam_ref[0] * o1
    o = o * lax.rsqrt(jnp.mean(o * o, axis=0, keepdims=True) + LN_EPS) * (g_ref[...] * out_scale)
    o_ref[0] = o.T


def diff_attention(q, k_ctx, vt_ctx, k_lat, vt_lat, lam, subln_g, out_scale):
    bsz, t, dw = q.shape
    n_heads = dw // DA_V_DIM
    tq = _row_tile(t, ATTN_TQ)
    has_lat = k_lat is not None
    kv_specs = [pl.BlockSpec((1, k_ctx.shape[1], DA_V_DIM), lambda b, h, i: (b, 0, h)),
                pl.BlockSpec((1, 1, DA_V_DIM, vt_ctx.shape[-1]), lambda b, h, i: (b, 0, h, 0))]
    kv_args = [k_ctx, vt_ctx]
    if has_lat:
        assert vt_lat.shape[-1] == ATTN_KV
        kv_specs += [pl.BlockSpec((1, k_lat.shape[1], DA_V_DIM), lambda b, h, i: (b, 0, h)),
                     pl.BlockSpec((1, vt_lat.shape[1], DA_V_DIM, ATTN_KV), lambda b, h, i: (b, 0, h, 0))]
        kv_args += [k_lat, vt_lat]
        assert vt_lat.shape[1] % 2 == 0
    tk_max = ATTN_KV if has_lat else k_ctx.shape[1]
    assert k_ctx.shape[1] <= tk_max
    weights = [pltpu.VMEM((2, tk_max, tq), BF16), pltpu.VMEM((2, 1, tq), F32)]
    scratch = [pltpu.VMEM((2, 1, tq), F32), pltpu.VMEM((2, DA_V_DIM + ATTN_ONES, tq), F32)] + weights
    if has_lat:
        scratch += [pltpu.VMEM((2, ATTN_KV, tq), F32)] * 2 + weights
    return pl.pallas_call(
        functools.partial(_attn_body, out_scale=out_scale, has_lat=has_lat),
        grid=(bsz, n_heads, t // tq),
        in_specs=[pl.BlockSpec(memory_space=pltpu.SMEM),
                  pl.BlockSpec((1, tq, DA_V_DIM), lambda b, h, i: (b, i, h))] + kv_specs
                 + [pl.BlockSpec((DA_V_DIM, 1), lambda b, h, i: (0, 0))],
        out_specs=pl.BlockSpec((1, tq, DA_V_DIM), lambda b, h, i: (b, i, h)),
        out_shape=jax.ShapeDtypeStruct((bsz, t, dw), F32),
        scratch_shapes=scratch,
        compiler_params=_params("parallel", "parallel", "parallel"),
        name="diff_attention",
    )(lam, q, *kv_args, subln_g)


def _cvconv_body(x_ref, prev_ref, next_ref, w_ref, b_ref, g_ref, beta_ref, o_ref, xe_sc, xs_sc):
    i, n = pl.program_id(1), pl.num_programs(1)
    tm = x_ref.shape[1]
    _fill_padded(xe_sc, x_ref[0], prev_ref, next_ref, i, n, CV_HALO)
    n_taps = w_ref.shape[0]
    first = CV_HALO - (n_taps - 1) // 2
    rows = xs_sc.shape[1]
    for r in range(SUBLANES):
        xs_sc[r] = xe_sc[r:r + rows, :]

    for base in range(0, tm, CV_ROWS):
        y = b_ref[...]
        for j in range(n_taps):
            phase, start = (first + j) % SUBLANES, base + (first + j) // SUBLANES * SUBLANES
            y = y + w_ref[j:j + 1] * xs_sc[phase, start:start + CV_ROWS, :]
        y = _layer_norm(y, g_ref[...], beta_ref[...])
        o_ref[0, base:base + CV_ROWS, :] = y * jax.nn.sigmoid(y)


def conformer_conv(glu, conv_w, conv_b, ln_g, ln_b):
    bsz, t, cw = glu.shape
    tm = _row_tile(t, 512)
    vec = lambda a: a.reshape(1, cw)
    return pl.pallas_call(
        _cvconv_body,
        grid=(bsz, t // tm),
        in_specs=_halo_specs(t, tm, cw, CV_HALO, lambda i: i)
                 + [_const_spec(conv_w.shape)] + [_const_spec((1, cw))] * 3,
        out_specs=pl.BlockSpec((1, tm, cw), lambda b, i: (b, i, 0)),
        out_shape=jax.ShapeDtypeStruct(glu.shape, F32),
        scratch_shapes=[pltpu.VMEM((tm + 2 * CV_HALO, cw), F32),
                        pltpu.VMEM((SUBLANES, tm + 2 * CV_HALO - SUBLANES, cw), F32)],
        compiler_params=_params("parallel", "parallel"),
        name="conformer_conv",
    )(glu, glu, glu, conv_w, vec(conv_b), vec(ln_g), vec(ln_b))


def _rglru_body(x_ref, prev_ref, next_ref, cw_ref, cb_ref, wa_ref, ba_ref, wx_ref, bx_ref, lam_ref, h0_ref,
                h_ref, hlast_ref, xe_sc, a_sc, b_sc, state_sc, *, reverse):
    i, n = pl.program_id(1), pl.num_programs(1)
    ti = n - 1 - i if reverse else i
    tm = x_ref.shape[1]

    @pl.when(i == 0)
    def _():
        state_sc[...] = h0_ref[0]

    _fill_padded(xe_sc, x_ref[0], prev_ref, next_ref, ti, n, HALO)
    u = _short_conv(xe_sc, cw_ref, cb_ref, tm)
    ub = u.astype(BF16)
    for c in range(wa_ref.shape[0]):
        sl = slice(MXU_TILE * c, MXU_TILE * (c + 1))
        r = jax.nn.sigmoid(_dot(ub[:, sl], wa_ref[c]) + ba_ref[:, sl])
        gate_in = jax.nn.sigmoid(_dot(ub[:, sl], wx_ref[c]) + bx_ref[:, sl])
        log_a = (-RG_C) * r * _softplus(-lam_ref[:, sl])
        a = jnp.exp(log_a)
        a_sc[:, sl] = a
        b_sc[:, sl] = jnp.sqrt(-jnp.tanh(log_a) * (a * a + 1.0)) * gate_in * u[:, sl]

    def step(s, h):
        row = tm - 1 - s if reverse else s
        h = a_sc[pl.ds(row, 1), :] * h + b_sc[pl.ds(row, 1), :]
        h_ref[0, pl.ds(row, 1), :] = h
        return h

    h = lax.fori_loop(0, tm, step, state_sc[...], unroll=8)
    state_sc[...] = h

    @pl.when(i == n - 1)
    def _():
        hlast_ref[0] = h


def rglru_direction(rg_x, conv_w, conv_b, w_a, b_a, w_x, b_x, lam, h0, reverse):
    bsz, t, width = rg_x.shape
    tm = _row_tile(t, 512)
    n = t // tm
    tidx = (lambda i: n - 1 - i) if reverse else (lambda i: i)
    vec = _const_spec((1, width))
    state_spec = pl.BlockSpec((1, 1, width), lambda b, i: (b, 0, 0))
    return pl.pallas_call(
        functools.partial(_rglru_body, reverse=reverse),
        grid=(bsz, n),
        in_specs=_halo_specs(t, tm, width, HALO, tidx)
                 + [_const_spec(conv_w.shape), vec, _const_spec(w_a.shape), vec, _const_spec(w_x.shape), vec, vec,
                    state_spec],
        out_specs=[pl.BlockSpec((1, tm, width), lambda b, i: (b, tidx(i), 0)), state_spec],
        out_shape=[jax.ShapeDtypeStruct(rg_x.shape, F32), jax.ShapeDtypeStruct((bsz, 1, width), F32)],
        scratch_shapes=[pltpu.VMEM((tm + 2 * HALO, width), F32), pltpu.VMEM((tm, width), F32),
                        pltpu.VMEM((tm, width), F32), pltpu.VMEM((1, width), F32)],
        compiler_params=_params("parallel", "arbitrary"),
        name="rglru_rev" if reverse else "rglru_fwd",
    )(rg_x, rg_x, rg_x, conv_w, conv_b, w_a, b_a, w_x, b_x, lam, h0)


def _split3(a):
    hi = a.astype(BF16)
    rest = a - hi.astype(F32)
    mid = rest.astype(BF16)
    return hi, mid, (rest - mid.astype(F32)).astype(BF16)


def _mlprep_body(x_ref, prev_ref, next_ref, cw_ref, cb_ref, wq_ref, wk_ref, wv_ref, wvt_ref, wg_ref, wgt_ref,
                 bg_ref, bgt_ref, tril_ref, triu_ref,
                 q_ref, k_ref, v_ref, vt_ref, mlc_ref, gcol_ref, grow_ref, xe_sc, *, k_scale):
    i, n = pl.program_id(1), pl.num_programs(1)
    tm = x_ref.shape[1]
    x = x_ref[0]
    _fill_padded(xe_sc, x, prev_ref, next_ref, i, n, HALO)
    conv = _short_conv(xe_sc, cw_ref, cb_ref, tm)
    ml_c = conv * jax.nn.sigmoid(conv)
    mlc_ref[0] = ml_c
    cb16, xb16 = ml_c.astype(BF16), x.astype(BF16)
    width = x.shape[1]
    n_gate = wg_ref.shape[1]
    pre = jnp.zeros((tm, n_gate), F32) + bg_ref[...]
    pre_t = jnp.zeros((n_gate, tm), F32) + bgt_ref[...]
    for c in range(wq_ref.shape[0]):
        sl = slice(MXU_TILE * c, MXU_TILE * (c + 1))
        q = _dot(cb16[:, sl], wq_ref[c]).astype(BF16)
        k = _dot(cb16[:, sl], wk_ref[c]).astype(BF16)
        v = _dot(xb16[:, sl], wv_ref[c]).astype(BF16)
        q_ref[0, :, sl] = q
        k_ref[0, :, sl] = k * k_scale
        v_ref[0, :, sl] = v
        vt_ref[0, sl, :] = _dot_nt(wvt_ref[c], xb16[:, sl]).astype(BF16)
        for part, val in enumerate((q, k, v)):
            rows = slice(part * width + MXU_TILE * c, part * width + MXU_TILE * (c + 1))
            pre = pre + _dot(val, wg_ref[rows, :])
            pre_t = pre_t + _dot_nt(wgt_ref[:, rows], val)
    half = n_gate // 2
    col = lax.broadcasted_iota(jnp.int32, (ML_CHUNK, n_gate), 1)
    row = lax.broadcasted_iota(jnp.int32, (n_gate, ML_CHUNK), 0)
    for c in range(tm // ML_CHUNK):
        ch = slice(ML_CHUNK * c, ML_CHUNK * (c + 1))
        pre_c, pre_tc = pre[ch], pre_t[:, ch]
        logf_c = jnp.where(col >= half, -_softplus(-pre_c), 0.0)
        logf_tc = jnp.where(row >= half, -_softplus(-pre_tc), 0.0)
        pieces, pieces_t = _split3(logf_c), _split3(logf_tc)
        prefix = sum(_dot(tril_ref[...], p) for p in pieces)
        suffix = sum(_dot(triu_ref[...], p) for p in pieces)
        prefix_t = sum(_dot(p, triu_ref[...]) for p in pieces_t)
        suffix_t = sum(_dot(p, tril_ref[...]) for p in pieces_t)
        gcol_ref[0, ch, :] = jnp.where(col < half, pre_c, jnp.where(col < half + half // 2, prefix, suffix))
        grow_ref[0, :, ch] = jnp.where(row < half, pre_tc, jnp.where(row < half + half // 2, prefix_t, suffix_t))


def mlstm_prep(ml_x, conv_w, conv_b, wq, wk, wv, wvt, wg, wgt, bg, bgt, tril, triu, k_scale):
    bsz, t, width = ml_x.shape
    tm = _row_tile(t, 512)
    n_gate = wg.shape[1]
    row = pl.BlockSpec((1, tm, width), lambda b, i: (b, i, 0))
    consts = (conv_w, conv_b, wq, wk, wv, wvt, wg, wgt, bg, bgt, tril, triu)
    return pl.pallas_call(
        functools.partial(_mlprep_body, k_scale=k_scale),
        grid=(bsz, t // tm),
        in_specs=_halo_specs(t, tm, width, HALO, lambda i: i) + [_const_spec(a.shape) for a in consts],
        out_specs=[row, row, row, pl.BlockSpec((1, width, tm), lambda b, i: (b, 0, i)), row,
                   pl.BlockSpec((1, tm, n_gate), lambda b, i: (b, i, 0)),
                   pl.BlockSpec((1, n_gate, tm), lambda b, i: (b, 0, i))],
        out_shape=[jax.ShapeDtypeStruct(ml_x.shape, BF16)] * 3
                  + [jax.ShapeDtypeStruct((bsz, width, t), BF16), jax.ShapeDtypeStruct(ml_x.shape, F32),
                     jax.ShapeDtypeStruct((bsz, t, n_gate), F32), jax.ShapeDtypeStruct((bsz, n_gate, t), F32)],
        scratch_shapes=[pltpu.VMEM((tm + 2 * HALO, width), F32)],
        compiler_params=_params("parallel", "parallel"),
        name="mlstm_prep",
    )(ml_x, ml_x, ml_x, *consts)


def _mlstm_body(q_ref, k_ref, v_ref, vt_ref, gcol_ref, grow_ref, c0_ref, n0_ref, m0_ref,
                h_ref, cf_ref, nf_ref, mf_ref, c_sc, n_sc, m_sc, *, reverse):
    i, n = pl.program_id(1), pl.num_programs(1)
    length = q_ref.shape[1]
    n_heads = c_sc.shape[0]
    dh = c_sc.shape[1]

    @pl.when(i == 0)
    def _():
        c_sc[...] = c0_ref[0]
        n_sc[...] = n0_ref[0]
        m_sc[...] = m0_ref[0]

    direction = 1 if reverse else 0
    t_idx = lax.broadcasted_iota(jnp.int32, (length, length), 0)
    s_idx = lax.broadcasted_iota(jnp.int32, (length, length), 1)
    visible = (s_idx >= t_idx) if reverse else (s_idx <= t_idx)
    for h in range(n_heads):
        sl = slice(dh * h, dh * (h + 1))
        ci = n_heads * direction + h
        cb = 2 * n_heads + ci
        qh, kh, vh, vth = q_ref[0, :, sl], k_ref[0, :, sl], v_ref[0, :, sl], vt_ref[0, sl, :]
        ig_row, b_row = grow_ref[0, ci:ci + 1, :], grow_ref[0, cb:cb + 1, :]
        b_col = gcol_ref[0, :, cb:cb + 1]
        g = b_col[0:1] if reverse else b_col[length - 1:length]
        m_st = m_sc[h]
        log_w = jnp.where(visible, b_col - b_row + ig_row, -jnp.inf)
        log_inter = b_col + m_st
        m_t = jnp.maximum(log_inter, jnp.max(log_w, axis=-1, keepdims=True))
        w = jnp.exp(log_w - m_t) * _dot_nt(qh, kh)
        s_inter = jnp.exp(log_inter - m_t)
        num = _dot(w.astype(BF16), vh) + s_inter * _dot_nt(qh, c_sc[h].astype(BF16))
        den = jnp.sum(w, axis=-1, keepdims=True) \
            + s_inter * jnp.sum(qh.astype(F32) * n_sc[h], axis=-1, keepdims=True)
        h_ref[0, :, sl] = num / jnp.maximum(jnp.abs(den), jnp.exp(-m_t))
        log_end = g - b_row + ig_row
        m_new = jnp.maximum(g + m_st, jnp.max(log_end, axis=-1, keepdims=True))
        e_row = jnp.exp(log_end - m_new)
        decay = jnp.exp(g + m_st - m_new)
        c_sc[h] = decay * c_sc[h] + _dot((vth * e_row).astype(BF16), kh)
        e8 = jnp.broadcast_to(e_row, (8, length)).astype(BF16)
        n_sc[h] = decay * n_sc[h] + _dot(e8, kh)[0:1]
        m_sc[h] = m_new

    @pl.when(i == n - 1)
    def _():
        cf_ref[0] = c_sc[...]
        nf_ref[0] = n_sc[...]
        mf_ref[0] = m_sc[...]


def mlstm_direction(q, k, v, vt, gcol, grow, state, reverse):
    bsz, t, width = q.shape
    n = t // ML_CHUNK
    n_gate = gcol.shape[-1]
    dh = width // ML_HEADS
    tidx = (lambda i: n - 1 - i) if reverse else (lambda i: i)
    row = pl.BlockSpec((1, ML_CHUNK, width), lambda b, i: (b, tidx(i), 0))
    state_specs = [pl.BlockSpec((1,) + s.shape[1:], lambda b, i: (b, 0, 0, 0)) for s in state]
    outs = pl.pallas_call(
        functools.partial(_mlstm_body, reverse=reverse),
        grid=(bsz, n),
        in_specs=[row, row, row, pl.BlockSpec((1, width, ML_CHUNK), lambda b, i: (b, 0, tidx(i))),
                  pl.BlockSpec((1, ML_CHUNK, n_gate), lambda b, i: (b, tidx(i), 0)),
                  pl.BlockSpec((1, n_gate, ML_CHUNK), lambda b, i: (b, 0, tidx(i)))] + state_specs,
        out_specs=[row] + state_specs,
        out_shape=[jax.ShapeDtypeStruct(q.shape, F32)] + [jax.ShapeDtypeStruct(s.shape, F32) for s in state],
        scratch_shapes=[pltpu.VMEM((ML_HEADS, dh, dh), F32), pltpu.VMEM((ML_HEADS, 1, dh), F32),
                        pltpu.VMEM((ML_HEADS, 1, 1), F32)],
        compiler_params=_params("parallel", "arbitrary"),
        name="mlstm_rev" if reverse else "mlstm_fwd",
    )(q, k, v, vt, gcol, grow, *state)
    return outs[0], tuple(outs[1:])


def _mixout_body(*refs, n_parts, alpha):
    x_ref, mod_ref, ln_ref = refs[:3]
    part_refs, w_refs, o_ref = refs[3:3 + n_parts], refs[3 + n_parts:3 + 2 * n_parts], refs[-1]
    x = x_ref[0]
    y = jnp.zeros(x.shape, F32)
    for p_ref, w_ref in zip(part_refs, w_refs):
        y = y + _dot(p_ref[0].astype(BF16), w_ref[...])
    z = alpha * x + mod_ref[0][2:3] * y
    o_ref[0] = _layer_norm(z, ln_ref[0:1], ln_ref[1:2])


def mixer_out(x, mod, ln, parts, weights, alpha):
    bsz, t, d = x.shape
    tm = _row_tile(t, 512)
    row = lambda w: pl.BlockSpec((1, tm, w), lambda b, i: (b, i, 0))
    return pl.pallas_call(
        functools.partial(_mixout_body, n_parts=len(parts), alpha=alpha),
        grid=(bsz, t // tm),
        in_specs=[row(d), _mod_spec(mod), _const_spec(ln.shape)] + [row(p.shape[-1]) for p in parts]
                 + [_const_spec(w.shape) for w in weights],
        out_specs=row(d),
        out_shape=jax.ShapeDtypeStruct(x.shape, F32),
        compiler_params=_params("parallel", "parallel"),
        name="mixer_out",
    )(x, mod, ln, *parts, *weights)


def _evenout_body(x_ref, mod_ref, ln_ref, rgf_ref, rgr_ref, gate_ref, mlf_ref, mlr_ref, mlc_ref, mlz_ref,
                  ng_ref, skip_ref, wrg_ref, wml_ref, o_ref, *, alpha):
    x = x_ref[0]
    y_rg = (rgf_ref[0] + rgr_ref[0]) * jax.nn.gelu(gate_ref[0], approximate=True)
    y = _dot(y_rg.astype(BF16), wrg_ref[...])
    h_ml = mlf_ref[0] + mlr_ref[0]
    z_gate = mlz_ref[0]
    dh = h_ml.shape[1] // ML_HEADS
    for h in range(ML_HEADS):
        sl = slice(dh * h, dh * (h + 1))
        hh = h_ml[:, sl]
        mu = jnp.mean(hh, axis=-1, keepdims=True)
        hc = hh - mu
        var = jnp.mean(hc * hc, axis=-1, keepdims=True)
        hn = hc * lax.rsqrt(var + LN_EPS) * ng_ref[:, sl]
        zg = z_gate[:, sl]
        y_ml = (hn + skip_ref[:, sl] * mlc_ref[0, :, sl]) * (zg * jax.nn.sigmoid(zg))
        y = y + _dot(y_ml.astype(BF16), wml_ref[sl, :])
    z = alpha * x + mod_ref[0][2:3] * y
    o_ref[0] = _layer_norm(z, ln_ref[0:1], ln_ref[1:2])


def even_out(x, mod, ln, streams, norm_g, skip, w_rg, w_ml, alpha):
    bsz, t, d = x.shape
    tm = _row_tile(t, 256)
    row = pl.BlockSpec((1, tm, d), lambda b, i: (b, i, 0))
    vec = _const_spec((1, d))
    return pl.pallas_call(
        functools.partial(_evenout_body, alpha=alpha),
        grid=(bsz, t // tm),
        in_specs=[row, _mod_spec(mod), _const_spec(ln.shape)] + [row] * len(streams)
                 + [vec, vec, _const_spec(w_rg.shape), _const_spec(w_ml.shape)],
        out_specs=row,
        out_shape=jax.ShapeDtypeStruct(x.shape, F32),
        compiler_params=_params("parallel", "parallel"),
        name="even_out",
    )(x, mod, ln, *streams, norm_g, skip, w_rg, w_ml)


def _dense_blocks(w, group):
    g, di, do = w.shape
    wb = w.reshape(g // group, group, di, do)
    eye = jnp.eye(group, dtype=w.dtype)
    return jnp.einsum('cgio,gh->cgiho', wb, eye).reshape(g // group, group * di, group * do).astype(BF16)


def _even_mixer(x_lat, x_ctx, mod_l, mod_c, ln, alpha, p):
    d = x_lat.shape[-1]
    bsz = x_lat.shape[0]
    w_in = p['w_in'].astype(BF16)
    streams = [inproj_even(x_lat, mod_l, w_in), inproj_even(x_ctx, mod_c, w_in)]
    width = streams[0][0].shape[-1]
    dh = width // ML_HEADS
    vec = lambda a: a.reshape(1, -1)
    rg_out = [[], []]
    for dr, rev in enumerate((False, True)):
        consts = (p['rg_conv_w'], vec(p['rg_conv_b']), _dense_blocks(p['rg_w_a'][dr], MXU_TILE // p['rg_w_a'].shape[-1]),
                  vec(p['rg_b_a'][dr]), _dense_blocks(p['rg_w_x'][dr], MXU_TILE // p['rg_w_x'].shape[-1]),
                  vec(p['rg_b_x'][dr]), vec(p['rg_lambda'][dr]))
        hc, h_end = rglru_direction(streams[1][0], *consts, jnp.zeros((bsz, 1, width), F32), rev)
        hl, _ = rglru_direction(streams[0][0], *consts, h_end, rev)
        rg_out[0].append(hl)
        rg_out[1].append(hc)
    group = MXU_TILE // p['ml_w_q'].shape[-1]
    wq, wk, wv = (_dense_blocks(p[n], group) for n in ('ml_w_q', 'ml_w_k', 'ml_w_v'))
    wvt = jnp.swapaxes(wv, 1, 2)
    wg_all, bg_all = p['ml_w_gate'], p['ml_b_gate']
    nh = ML_HEADS
    wg = jnp.concatenate([wg_all[0][:, :nh], wg_all[1][:, :nh], wg_all[0][:, nh:], wg_all[1][:, nh:]], axis=1)
    bg = jnp.concatenate([bg_all[0][:nh], bg_all[1][:nh], bg_all[0][nh:], bg_all[1][nh:]])
    tril = jnp.tril(jnp.ones((ML_CHUNK, ML_CHUNK), BF16))
    prep_consts = (p['ml_conv_w'], vec(p['ml_conv_b']), wq, wk, wv, wvt, wg.astype(BF16), wg.T.astype(BF16),
                   bg.reshape(1, -1), bg.reshape(-1, 1), tril, tril.T)
    preps = [mlstm_prep(s[2], *prep_consts, dh ** -0.5) for s in streams]
    ml_out = [[], []]
    for rev in (False, True):
        state0 = (jnp.zeros((bsz, nh, dh, dh), F32), jnp.zeros((bsz, nh, 1, dh), F32),
                  jnp.zeros((bsz, nh, 1, 1), F32))
        seq = lambda pr: pr[:4] + pr[5:]
        mc, st = mlstm_direction(*seq(preps[1]), state0, rev)
        ml, _ = mlstm_direction(*seq(preps[0]), st, rev)
        ml_out[0].append(ml)
        ml_out[1].append(mc)
    w_out = p['w_out'].astype(BF16)
    outs = []
    for side, (x_in, mod) in enumerate(((x_lat, mod_l), (x_ctx, mod_c))):
        rg_x, rg_gate, ml_x, ml_z = streams[side]
        outs.append(even_out(x_in, mod, ln, (rg_out[side][0], rg_out[side][1], rg_gate, ml_out[side][0],
                                             ml_out[side][1], preps[side][4], ml_z),
                             vec(p['ml_norm_g']), vec(p['ml_skip']), w_out[:width], w_out[width:], alpha))
    return outs


def kernel(x, c, ctx, c_ctx, w_ada, b_ada, ln_g, ln_b, ffn_w_gate, ffn_w_up, ffn_w_down, ev_w_in, ev_w_out, rg_conv_w, rg_conv_b, rg_w_a, rg_b_a, rg_w_x, rg_b_x, rg_lambda, ml_conv_w, ml_conv_b, ml_w_q, ml_w_k, ml_w_v, ml_w_gate, ml_b_gate, ml_norm_g, ml_skip, od_w_in, od_w_out, cv_conv_w, cv_conv_b, cv_ln_g, cv_ln_b, da_lambda, da_subln_g):
    depth = w_ada.shape[0]
    d = x.shape[-1]
    alpha = (2 * depth) ** 0.25
    bsz, t_len = x.shape[:2]
    ctx_len = ctx.shape[1]
    rope = _rope_tables(t_len)
    x_lat, x_ctx = x, ctx
    cond_lat = jax.nn.silu(c)
    cond_ctx = jax.nn.silu(c_ctx)
    cw = cv_conv_w.shape[-1]
    dw = da_subln_g.shape[-1] * DA_HEADS
    for layer in range(depth):
        need_ctx = layer < depth - 1
        mod_l = (cond_lat @ w_ada[layer] + b_ada[layer]).reshape(bsz, N_SUB, 3, d)
        mod_c = (cond_ctx @ w_ada[layer] + b_ada[layer]).reshape(1, N_SUB, 3, d)
        ln = jnp.stack([ln_g[layer], ln_b[layer]], axis=1)
        ffn1 = _prep_ffn(ffn_w_gate[layer, 0], ffn_w_up[layer, 0], ffn_w_down[layer, 0])
        ffn2 = _prep_ffn(ffn_w_gate[layer, 1], ffn_w_up[layer, 1], ffn_w_down[layer, 1])
        x_lat = ffn_sublayer(x_lat, mod_l[:, 0], *ffn1, ln[0], alpha)
        x_ctx = ffn_sublayer(x_ctx, mod_c[:, 0], *ffn1, ln[0], alpha)
        i = layer // 2
        if layer % 2 == 0:
            p = dict(w_in=ev_w_in[i], w_out=ev_w_out[i], rg_conv_w=rg_conv_w[i], rg_conv_b=rg_conv_b[i],
                     rg_w_a=rg_w_a[i], rg_b_a=rg_b_a[i], rg_w_x=rg_w_x[i], rg_b_x=rg_b_x[i], rg_lambda=rg_lambda[i],
                     ml_conv_w=ml_conv_w[i], ml_conv_b=ml_conv_b[i], ml_w_q=ml_w_q[i], ml_w_k=ml_w_k[i],
                     ml_w_v=ml_w_v[i], ml_w_gate=ml_w_gate[i], ml_b_gate=ml_b_gate[i], ml_norm_g=ml_norm_g[i],
                     ml_skip=ml_skip[i])
            x_lat, x_ctx_new = _even_mixer(x_lat, x_ctx, mod_l[:, 1], mod_c[:, 1], ln[1], alpha, p)
        else:
            w_in = od_w_in[i].astype(BF16)
            w_vt = w_in[:, 2 * cw + 2 * dw:].T
            lam_init = 0.8 - 0.6 * math.exp(-0.3 * layer)
            lamf = da_lambda[i].astype(F32)
            lam = (jnp.exp(jnp.sum(lamf[0] * lamf[1])) - jnp.exp(jnp.sum(lamf[2] * lamf[3])) + lam_init).reshape(1)
            subln = da_subln_g[i].reshape(-1, 1)
            glu_l, q_l, k_l, vt_l = inproj_odd(x_lat, mod_l[:, 1], w_in, w_vt, rope, cw, dw)
            glu_c, q_c, k_c, vt_c = inproj_odd(x_ctx, mod_c[:, 1], w_in, w_vt, None, cw, dw)
            attn_l = diff_attention(q_l, k_c, vt_c, k_l, vt_l, lam, subln, 1.0 - lam_init)
            conv = (cv_conv_w[i], cv_conv_b[i], cv_ln_g[i], cv_ln_b[i])
            w_out = od_w_out[i].astype(BF16)
            w_parts = (w_out[:cw], w_out[cw:])
            x_lat = mixer_out(x_lat, mod_l[:, 1], ln[1], (conformer_conv(glu_l, *conv), attn_l), w_parts, alpha)
            if need_ctx:
                attn_c = diff_attention(q_c, k_c, vt_c, None, None, lam, subln, 1.0 - lam_init)
                x_ctx_new = mixer_out(x_ctx, mod_c[:, 1], ln[1], (conformer_conv(glu_c, *conv), attn_c), w_parts,
                                      alpha)
        x_lat = ffn_sublayer(x_lat, mod_l[:, 2], *ffn2, ln[2], alpha)
        if need_ctx:
            x_ctx = ffn_sublayer(x_ctx_new, mod_c[:, 2], *ffn2, ln[2], alpha)
    return x_lat
```

```python
import functools
import math

import jax
import jax.numpy as jnp
from jax import lax
from jax.experimental import pallas as pl
from jax.experimental.pallas import tpu as pltpu

F32 = jnp.float32
BF16 = jnp.bfloat16

GRID_W = 64
N_SUB = 3
HALF_STEP = 0.5
LN_EPS = 1e-5
RG_C = 8.0
ML_HEADS = 4
DA_HEADS = 8
DA_HEAD_DIM = 64
DA_V_DIM = 2 * DA_HEAD_DIM
ROPE_BASE = 10000.0
LOG2_E = 1.4426950408889634

FFN_CHUNK = 256
MXU_TILE = 256
ML_CHUNK = 256
HALO = 8
CV_HALO = 16
CV_ROWS = 32
SUBLANES = 8
ATTN_TQ = 1024
ATTN_KV = 512
ATTN_SUB = 1
ATTN_ONES = 16
VMEM_LIMIT = 56 * 1024 * 1024


def _params(*sem):
    return pltpu.CompilerParams(dimension_semantics=sem, vmem_limit_bytes=VMEM_LIMIT)


def _row_tile(t, want):
    tm = min(t, want)
    assert t % tm == 0
    return tm


def _layer_norm(z, g, b):
    mu = jnp.mean(z, axis=-1, keepdims=True)
    zc = z - mu
    var = jnp.mean(zc * zc, axis=-1, keepdims=True)
    return zc * lax.rsqrt(var + LN_EPS) * g + b


def _softplus(z):
    return jnp.maximum(z, 0.0) + jnp.log1p(jnp.exp(-jnp.abs(z)))


def _dot(a, b):
    return jnp.dot(a, b, preferred_element_type=F32)


def _dot_nt(a, b):
    return lax.dot_general(a, b, (((1,), (1,)), ((), ())), preferred_element_type=F32)


def _mod_spec(mod):
    per_sample = mod.shape[0] > 1
    return pl.BlockSpec((1,) + mod.shape[1:], lambda b, i: (b if per_sample else 0, 0, 0))


def _const_spec(shape):
    return pl.BlockSpec(shape, lambda b, i: (0,) * len(shape))


def _halo_specs(t, tm, width, halo, time_index):
    per_tile, n_halo = tm // halo, t // halo
    cur = pl.BlockSpec((1, tm, width), lambda b, i: (b, time_index(i), 0))
    prev = pl.BlockSpec((1, halo, width), lambda b, i: (b, jnp.maximum(time_index(i) * per_tile - 1, 0), 0))
    nxt = pl.BlockSpec((1, halo, width),
                       lambda b, i: (b, jnp.minimum((time_index(i) + 1) * per_tile, n_halo - 1), 0))
    return [cur, prev, nxt]


def _fill_padded(xe_sc, cur, prev_ref, next_ref, ti, n_tiles, halo):
    tm = cur.shape[0]
    xe_sc[0:halo] = prev_ref[0] * (ti > 0).astype(F32)
    xe_sc[halo:halo + tm] = cur
    xe_sc[halo + tm:2 * halo + tm] = next_ref[0] * (ti < n_tiles - 1).astype(F32)


def _short_conv(xe_sc, w_ref, b_ref, tm):
    y = b_ref[...] + w_ref[0:1] * xe_sc[pl.ds(HALO - 1, tm), :]
    for j in range(1, w_ref.shape[0]):
        y = y + w_ref[j:j + 1] * xe_sc[pl.ds(HALO - 1 + j, tm), :]
    return y


def _ffn_body(x_ref, mod_ref, wg_ref, wu_ref, wd_ref, ln_ref, o_ref, *, alpha):
    x = x_ref[0]
    mod = mod_ref[0]
    h = (x * (1.0 + mod[1:2]) + mod[0:1]).astype(BF16)
    y = jnp.zeros(x.shape, F32)
    for c in range(0, wg_ref.shape[1], FFN_CHUNK):
        g = _dot(h, wg_ref[:, c:c + FFN_CHUNK])
        u = _dot(h, wu_ref[:, c:c + FFN_CHUNK])
        a = (g * jax.nn.sigmoid(g) * u).astype(BF16)
        y = y + _dot(a, wd_ref[c:c + FFN_CHUNK, :])
    z = alpha * x + (HALF_STEP * mod[2:3]) * y
    o_ref[0] = _layer_norm(z, ln_ref[0:1], ln_ref[1:2])


def ffn_sublayer(x, mod, wg, wu, wd, ln, alpha):
    bsz, t, d = x.shape
    tm = _row_tile(t, 512)
    return pl.pallas_call(
        functools.partial(_ffn_body, alpha=alpha),
        grid=(bsz, t // tm),
        in_specs=[pl.BlockSpec((1, tm, d), lambda b, i: (b, i, 0)), _mod_spec(mod),
                  _const_spec(wg.shape), _const_spec(wu.shape), _const_spec(wd.shape), _const_spec(ln.shape)],
        out_specs=pl.BlockSpec((1, tm, d), lambda b, i: (b, i, 0)),
        out_shape=jax.ShapeDtypeStruct(x.shape, F32),
        compiler_params=_params("parallel", "parallel"),
        name="ffn_sublayer",
    )(x, mod, wg, wu, wd, ln)


def _prep_ffn(wg, wu, wd):
    assert wg.shape[1] % FFN_CHUNK == 0
    return wg.astype(BF16), wu.astype(BF16), wd.astype(BF16)


def _inproj_even_body(x_ref, mod_ref, w_ref, *o_refs):
    mod = mod_ref[0]
    h = (x_ref[0] * (1.0 + mod[1:2]) + mod[0:1]).astype(BF16)
    width = o_refs[0].shape[-1]
    for j, o_ref in enumerate(o_refs):
        o_ref[0] = _dot(h, w_ref[:, j * width:(j + 1) * width])


def inproj_even(x, mod, w_in):
    bsz, t, d = x.shape
    tm = _row_tile(t, 512)
    n_out = w_in.shape[1] // d
    row_spec = pl.BlockSpec((1, tm, d), lambda b, i: (b, i, 0))
    return pl.pallas_call(
        _inproj_even_body,
        grid=(bsz, t // tm),
        in_specs=[row_spec, _mod_spec(mod), _const_spec(w_in.shape)],
        out_specs=[row_spec] * n_out,
        out_shape=[jax.ShapeDtypeStruct(x.shape, F32)] * n_out,
        compiler_params=_params("parallel", "parallel"),
        name="inproj_even",
    )(x, mod, w_in)


def _rope128(a, cos, sin_lo, sin_hi):
    return a * cos + pltpu.roll(a, 112, 1) * sin_lo + pltpu.roll(a, 16, 1) * sin_hi


def _inproj_odd_body(x_ref, mod_ref, w_ref, wvt_ref, *rest, use_rope):
    if use_rope:
        cos_ref, slo_ref, shi_ref, glu_ref, q_ref, k_ref, vt_ref = rest
    else:
        glu_ref, q_ref, k_ref, vt_ref = rest
    mod = mod_ref[0]
    h = (x_ref[0] * (1.0 + mod[1:2]) + mod[0:1]).astype(BF16)
    cw = glu_ref.shape[-1]
    a = _dot(h, w_ref[:, :cw])
    gate = _dot(h, w_ref[:, cw:2 * cw])
    glu_ref[0] = a * jax.nn.sigmoid(gate)
    dw = q_ref.shape[-1]
    q = _dot(h, w_ref[:, 2 * cw:2 * cw + dw]) * (DA_HEAD_DIM ** -0.5 * LOG2_E)
    k = _dot(h, w_ref[:, 2 * cw + dw:2 * cw + 2 * dw])
    if use_rope:
        cos, slo, shi = cos_ref[...], slo_ref[...], shi_ref[...]
        for j in range(dw // 128):
            sl = slice(128 * j, 128 * (j + 1))
            q_ref[0, :, sl] = _rope128(q[:, sl], cos, slo, shi).astype(BF16)
            k_ref[0, :, sl] = _rope128(k[:, sl], cos, slo, shi).astype(BF16)
    else:
        q_ref[0] = q.astype(BF16)
        k_ref[0] = k.astype(BF16)
    vt_ref[0, 0] = _dot_nt(wvt_ref[...], h).astype(BF16)


def inproj_odd(x, mod, w_in, w_vt, rope, cw, dw):
    bsz, t, d = x.shape
    tm = _row_tile(t, ATTN_KV)
    row = lambda w: pl.BlockSpec((1, tm, w), lambda b, i: (b, i, 0))
    in_specs = [row(d), _mod_spec(mod), _const_spec(w_in.shape), _const_spec(w_vt.shape)]
    args = [x, mod, w_in, w_vt]
    if rope is not None:
        in_specs += [pl.BlockSpec((tm, 128), lambda b, i: (i, 0))] * 3
        args += list(rope)
    return pl.pallas_call(
        functools.partial(_inproj_odd_body, use_rope=rope is not None),
        grid=(bsz, t // tm),
        in_specs=in_specs,
        out_specs=[row(cw), row(dw), row(dw), pl.BlockSpec((1, 1, dw, tm), lambda b, i: (b, i, 0, 0))],
        out_shape=[jax.ShapeDtypeStruct((bsz, t, cw), F32)] + [jax.ShapeDtypeStruct((bsz, t, dw), BF16)] * 2
                  + [jax.ShapeDtypeStruct((bsz, t // tm, dw, tm), BF16)],
        compiler_params=_params("parallel", "parallel"),
        name="inproj_odd",
    )(*args)


def _rope_tables(t):
    pos = jnp.arange(t)
    row = (pos // GRID_W).astype(F32)
    col = (pos % GRID_W).astype(F32)
    axis_dim = DA_HEAD_DIM // 2
    inv_freq = ROPE_BASE ** (-jnp.arange(0, axis_dim, 2, dtype=F32) / axis_dim)
    ang_r = row[:, None] * inv_freq
    ang_c = col[:, None] * inv_freq
    zeros = jnp.zeros_like(ang_r)
    cos64 = jnp.concatenate([jnp.cos(ang_r)] * 2 + [jnp.cos(ang_c)] * 2, axis=-1)
    sin_lo64 = jnp.concatenate([-jnp.sin(ang_r), zeros, -jnp.sin(ang_c), zeros], axis=-1)
    sin_hi64 = jnp.concatenate([zeros, jnp.sin(ang_r), zeros, jnp.sin(ang_c)], axis=-1)
    return tuple(jnp.tile(a, (1, 2)) for a in (cos64, sin_lo64, sin_hi64))


def _attn_body(lam_ref, q_ref, kc_ref, vtc_ref, *rest, out_scale, has_lat):
    if has_lat:
        kl_ref, vtl_ref, g_ref, o_ref, m_sc, acc_sc, pa_sc, ca_sc, sa_sc, sb_sc, pb_sc, cb_sc = rest
    else:
        g_ref, o_ref, m_sc, acc_sc, pa_sc, ca_sc = rest
    q = q_ref[0]
    lane = lax.broadcasted_iota(jnp.int32, q.shape, 1)
    zero = jnp.zeros_like(q)
    q_maps = (jnp.where(lane < DA_HEAD_DIM, q, zero), jnp.where(lane >= DA_HEAD_DIM, q, zero))
    m_sc[...] = jnp.full(m_sc.shape, -jnp.inf, F32)
    acc_sc[...] = jnp.zeros(acc_sc.shape, F32)

    def scores(k):
        return [_dot_nt(k, q_maps[m]) for m in range(2)]

    def softmax(st_maps, p_sc, c_sc):
        tk = st_maps[0].shape[0]
        for m in range(2):
            m_prev = m_sc[m]
            m_new = jnp.maximum(m_prev, jnp.max(st_maps[m], axis=0, keepdims=True))
            p_sc[m, 0:tk] = jnp.exp2(st_maps[m] - m_new).astype(BF16)
            c_sc[m] = jnp.exp2(m_prev - m_new)
            m_sc[m] = m_new

    def accumulate(vt, p_sc, c_sc):
        tk = vt.shape[1]
        vt_ext = jnp.concatenate([vt, jnp.ones((ATTN_ONES, tk), BF16)], axis=0)
        for m in range(2):
            acc_sc[m] = c_sc[m] * acc_sc[m] + _dot(vt_ext, p_sc[m, 0:tk])

    softmax(scores(kc_ref[0]), pa_sc, ca_sc)
    accumulate(vtc_ref[0, 0], pa_sc, ca_sc)
    if has_lat:
        block = sa_sc.shape[1]
        sub = block // ATTN_KV
        n_blocks = vtl_ref.shape[1] // sub

        def k_block(j):
            return kl_ref[0, pl.ds(pl.multiple_of(j * block, block), block), :]

        def vt_block(j):
            return jnp.concatenate([vtl_ref[0, j * sub + s] for s in range(sub)], axis=1)

        def put(dst_sc, st_maps):
            dst_sc[0], dst_sc[1] = st_maps

        put(sa_sc, scores(k_block(0)))
        pb_sc[...] = jnp.zeros(pb_sc.shape, BF16)
        cb_sc[...] = jnp.ones(cb_sc.shape, F32)

        def step(jj, carry):
            j = 2 * jj
            put(sb_sc, scores(k_block(j + 1)))
            accumulate(vt_block(jnp.maximum(j - 1, 0)), pb_sc, cb_sc)
            softmax((sa_sc[0], sa_sc[1]), pa_sc, ca_sc)
            put(sa_sc, scores(k_block(jnp.minimum(j + 2, n_blocks - 1))))
            accumulate(vt_block(j), pa_sc, ca_sc)
            softmax((sb_sc[0], sb_sc[1]), pb_sc, cb_sc)
            return carry

        lax.fori_loop(0, n_blocks // 2, step, 0)
        accumulate(vt_block(n_blocks - 1), pb_sc, cb_sc)

    o0 = acc_sc[0, :DA_V_DIM] / acc_sc[0, DA_V_DIM:DA_V_DIM + 1]
    o1 = acc_sc[1, :DA_V_DIM] / acc_sc[1, DA_V_DIM:DA_V_DIM + 1]
    o = o0 - lam_ref[0] * o1
    o = o * lax.rsqrt(jnp.mean(o * o, axis=0, keepdims=True) + LN_EPS) * (g_ref[...] * out_scale)
    o_ref[0] = o.T


def diff_attention(q, k_ctx, vt_ctx, k_lat, vt_lat, lam, subln_g, out_scale):
    bsz, t, dw = q.shape
    n_heads = dw // DA_V_DIM
    tq = _row_tile(t, ATTN_TQ)
    has_lat = k_lat is not None
    kv_specs = [pl.BlockSpec((1, k_ctx.shape[1], DA_V_DIM), lambda b, h, i: (b, 0, h)),
                pl.BlockSpec((1, 1, DA_V_DIM, vt_ctx.shape[-1]), lambda b, h, i: (b, 0, h, 0))]
    kv_args = [k_ctx, vt_ctx]
    if has_lat:
        assert vt_lat.shape[-1] == ATTN_KV
        kv_specs += [pl.BlockSpec((1, k_lat.shape[1], DA_V_DIM), lambda b, h, i: (b, 0, h)),
                     pl.BlockSpec((1, vt_lat.shape[1], DA_V_DIM, ATTN_KV), lambda b, h, i: (b, 0, h, 0))]
        kv_args += [k_lat, vt_lat]
        sub = ATTN_SUB if vt_lat.shape[1] % (2 * ATTN_SUB) == 0 else 1
        assert vt_lat.shape[1] % (2 * sub) == 0
    tk_max = sub * ATTN_KV if has_lat else k_ctx.shape[1]
    assert k_ctx.shape[1] <= tk_max
    weights = [pltpu.VMEM((2, tk_max, tq), BF16), pltpu.VMEM((2, 1, tq), F32)]
    scratch = [pltpu.VMEM((2, 1, tq), F32), pltpu.VMEM((2, DA_V_DIM + ATTN_ONES, tq), F32)] + weights
    if has_lat:
        scratch += [pltpu.VMEM((2, tk_max, tq), F32)] * 2 + weights
    return pl.pallas_call(
        functools.partial(_attn_body, out_scale=out_scale, has_lat=has_lat),
        grid=(bsz, n_heads, t // tq),
        in_specs=[pl.BlockSpec(memory_space=pltpu.SMEM),
                  pl.BlockSpec((1, tq, DA_V_DIM), lambda b, h, i: (b, i, h))] + kv_specs
                 + [pl.BlockSpec((DA_V_DIM, 1), lambda b, h, i: (0, 0))],
        out_specs=pl.BlockSpec((1, tq, DA_V_DIM), lambda b, h, i: (b, i, h)),
        out_shape=jax.ShapeDtypeStruct((bsz, t, dw), F32),
        scratch_shapes=scratch,
        compiler_params=_params("parallel", "parallel", "parallel"),
        name="diff_attention",
    )(lam, q, *kv_args, subln_g)


def _cvconv_body(x_ref, prev_ref, next_ref, w_ref, b_ref, g_ref, beta_ref, o_ref, xe_sc, xs_sc):
    i, n = pl.program_id(1), pl.num_programs(1)
    tm = x_ref.shape[1]
    _fill_padded(xe_sc, x_ref[0], prev_ref, next_ref, i, n, CV_HALO)
    n_taps = w_ref.shape[0]
    first = CV_HALO - (n_taps - 1) // 2
    rows = xs_sc.shape[1]
    for r in range(SUBLANES):
        xs_sc[r] = xe_sc[r:r + rows, :]

    for base in range(0, tm, CV_ROWS):
        y = b_ref[...]
        for j in range(n_taps):
            phase, start = (first + j) % SUBLANES, base + (first + j) // SUBLANES * SUBLANES
            y = y + w_ref[j:j + 1] * xs_sc[phase, start:start + CV_ROWS, :]
        y = _layer_norm(y, g_ref[...], beta_ref[...])
        o_ref[0, base:base + CV_ROWS, :] = y * jax.nn.sigmoid(y)


def conformer_conv(glu, conv_w, conv_b, ln_g, ln_b):
    bsz, t, cw = glu.shape
    tm = _row_tile(t, 512)
    vec = lambda a: a.reshape(1, cw)
    return pl.pallas_call(
        _cvconv_body,
        grid=(bsz, t // tm),
        in_specs=_halo_specs(t, tm, cw, CV_HALO, lambda i: i)
                 + [_const_spec(conv_w.shape)] + [_const_spec((1, cw))] * 3,
        out_specs=pl.BlockSpec((1, tm, cw), lambda b, i: (b, i, 0)),
        out_shape=jax.ShapeDtypeStruct(glu.shape, F32),
        scratch_shapes=[pltpu.VMEM((tm + 2 * CV_HALO, cw), F32),
                        pltpu.VMEM((SUBLANES, tm + 2 * CV_HALO - SUBLANES, cw), F32)],
        compiler_params=_params("parallel", "parallel"),
        name="conformer_conv",
    )(glu, glu, glu, conv_w, vec(conv_b), vec(ln_g), vec(ln_b))


def _rglru_body(x_ref, prev_ref, next_ref, cw_ref, cb_ref, wa_ref, ba_ref, wx_ref, bx_ref, lam_ref, h0_ref,
                *rest, reverse, has_add):
    add_ref = rest[0] if has_add else None
    h_ref, hlast_ref, xe_sc, a_sc, b_sc, state_sc = rest[1:] if has_add else rest
    i, n = pl.program_id(1), pl.num_programs(1)
    ti = n - 1 - i if reverse else i
    tm = x_ref.shape[1]

    @pl.when(i == 0)
    def _():
        state_sc[...] = h0_ref[0]

    _fill_padded(xe_sc, x_ref[0], prev_ref, next_ref, ti, n, HALO)
    u = _short_conv(xe_sc, cw_ref, cb_ref, tm)
    ub = u.astype(BF16)
    for c in range(wa_ref.shape[0]):
        sl = slice(MXU_TILE * c, MXU_TILE * (c + 1))
        r = jax.nn.sigmoid(_dot(ub[:, sl], wa_ref[c]) + ba_ref[:, sl])
        gate_in = jax.nn.sigmoid(_dot(ub[:, sl], wx_ref[c]) + bx_ref[:, sl])
        log_a = (-RG_C) * r * _softplus(-lam_ref[:, sl])
        a = jnp.exp(log_a)
        a_sc[:, sl] = a
        b_sc[:, sl] = jnp.sqrt(-jnp.tanh(log_a) * (a * a + 1.0)) * gate_in * u[:, sl]

    def step(s, h):
        row = tm - 1 - s if reverse else s
        h = a_sc[pl.ds(row, 1), :] * h + b_sc[pl.ds(row, 1), :]
        h_ref[0, pl.ds(row, 1), :] = h
        return h

    h = lax.fori_loop(0, tm, step, state_sc[...], unroll=8)
    state_sc[...] = h
    if has_add:
        h_ref[0] = h_ref[0] + add_ref[0]

    @pl.when(i == n - 1)
    def _():
        hlast_ref[0] = h


def rglru_direction(rg_x, conv_w, conv_b, w_a, b_a, w_x, b_x, lam, h0, reverse, add=None):
    bsz, t, width = rg_x.shape
    tm = _row_tile(t, 512)
    n = t // tm
    tidx = (lambda i: n - 1 - i) if reverse else (lambda i: i)
    vec = _const_spec((1, width))
    state_spec = pl.BlockSpec((1, 1, width), lambda b, i: (b, 0, 0))
    tile_spec = pl.BlockSpec((1, tm, width), lambda b, i: (b, tidx(i), 0))
    extra = () if add is None else (add,)
    return pl.pallas_call(
        functools.partial(_rglru_body, reverse=reverse, has_add=add is not None),
        grid=(bsz, n),
        in_specs=_halo_specs(t, tm, width, HALO, tidx)
                 + [_const_spec(conv_w.shape), vec, _const_spec(w_a.shape), vec, _const_spec(w_x.shape), vec, vec,
                    state_spec] + [tile_spec] * len(extra),
        out_specs=[tile_spec, state_spec],
        out_shape=[jax.ShapeDtypeStruct(rg_x.shape, F32), jax.ShapeDtypeStruct((bsz, 1, width), F32)],
        scratch_shapes=[pltpu.VMEM((tm + 2 * HALO, width), F32), pltpu.VMEM((tm, width), F32),
                        pltpu.VMEM((tm, width), F32), pltpu.VMEM((1, width), F32)],
        compiler_params=_params("parallel", "arbitrary"),
        name="rglru_rev" if reverse else "rglru_fwd",
    )(rg_x, rg_x, rg_x, conv_w, conv_b, w_a, b_a, w_x, b_x, lam, h0, *extra)


def _split3(a):
    hi = a.astype(BF16)
    rest = a - hi.astype(F32)
    mid = rest.astype(BF16)
    return hi, mid, (rest - mid.astype(F32)).astype(BF16)


def _mlprep_body(x_ref, prev_ref, next_ref, cw_ref, cb_ref, wq_ref, wk_ref, wv_ref, wvt_ref, wg_ref, wgt_ref,
                 bg_ref, bgt_ref, tril_ref, triu_ref,
                 q_ref, k_ref, v_ref, vt_ref, mlc_ref, gcol_ref, grow_ref, xe_sc, *, k_scale):
    i, n = pl.program_id(1), pl.num_programs(1)
    tm = x_ref.shape[1]
    x = x_ref[0]
    _fill_padded(xe_sc, x, prev_ref, next_ref, i, n, HALO)
    conv = _short_conv(xe_sc, cw_ref, cb_ref, tm)
    ml_c = conv * jax.nn.sigmoid(conv)
    mlc_ref[0] = ml_c
    cb16, xb16 = ml_c.astype(BF16), x.astype(BF16)
    width = x.shape[1]
    n_gate = wg_ref.shape[1]
    pre = jnp.zeros((tm, n_gate), F32) + bg_ref[...]
    pre_t = jnp.zeros((n_gate, tm), F32) + bgt_ref[...]
    for c in range(wq_ref.shape[0]):
        sl = slice(MXU_TILE * c, MXU_TILE * (c + 1))
        q = _dot(cb16[:, sl], wq_ref[c]).astype(BF16)
        k = _dot(cb16[:, sl], wk_ref[c]).astype(BF16)
        v = _dot(xb16[:, sl], wv_ref[c]).astype(BF16)
        q_ref[0, :, sl] = q
        k_ref[0, :, sl] = k * k_scale
        v_ref[0, :, sl] = v
        vt_ref[0, sl, :] = _dot_nt(wvt_ref[c], xb16[:, sl]).astype(BF16)
        for part, val in enumerate((q, k, v)):
            rows = slice(part * width + MXU_TILE * c, part * width + MXU_TILE * (c + 1))
            pre = pre + _dot(val, wg_ref[rows, :])
            pre_t = pre_t + _dot_nt(wgt_ref[:, rows], val)
    half = n_gate // 2
    col = lax.broadcasted_iota(jnp.int32, (ML_CHUNK, n_gate), 1)
    row = lax.broadcasted_iota(jnp.int32, (n_gate, ML_CHUNK), 0)
    for c in range(tm // ML_CHUNK):
        ch = slice(ML_CHUNK * c, ML_CHUNK * (c + 1))
        pre_c, pre_tc = pre[ch], pre_t[:, ch]
        logf_c = jnp.where(col >= half, -_softplus(-pre_c), 0.0)
        logf_tc = jnp.where(row >= half, -_softplus(-pre_tc), 0.0)
        pieces, pieces_t = _split3(logf_c), _split3(logf_tc)
        prefix = sum(_dot(tril_ref[...], p) for p in pieces)
        suffix = sum(_dot(triu_ref[...], p) for p in pieces)
        prefix_t = sum(_dot(p, triu_ref[...]) for p in pieces_t)
        suffix_t = sum(_dot(p, tril_ref[...]) for p in pieces_t)
        gcol_ref[0, ch, :] = jnp.where(col < half, pre_c, jnp.where(col < half + half // 2, prefix, suffix))
        grow_ref[0, :, ch] = jnp.where(row < half, pre_tc, jnp.where(row < half + half // 2, prefix_t, suffix_t))


def mlstm_prep(ml_x, conv_w, conv_b, wq, wk, wv, wvt, wg, wgt, bg, bgt, tril, triu, k_scale):
    bsz, t, width = ml_x.shape
    tm = _row_tile(t, 512)
    n_gate = wg.shape[1]
    row = pl.BlockSpec((1, tm, width), lambda b, i: (b, i, 0))
    consts = (conv_w, conv_b, wq, wk, wv, wvt, wg, wgt, bg, bgt, tril, triu)
    return pl.pallas_call(
        functools.partial(_mlprep_body, k_scale=k_scale),
        grid=(bsz, t // tm),
        in_specs=_halo_specs(t, tm, width, HALO, lambda i: i) + [_const_spec(a.shape) for a in consts],
        out_specs=[row, row, row, pl.BlockSpec((1, width, tm), lambda b, i: (b, 0, i)), row,
                   pl.BlockSpec((1, tm, n_gate), lambda b, i: (b, i, 0)),
                   pl.BlockSpec((1, n_gate, tm), lambda b, i: (b, 0, i))],
        out_shape=[jax.ShapeDtypeStruct(ml_x.shape, BF16)] * 3
                  + [jax.ShapeDtypeStruct((bsz, width, t), BF16), jax.ShapeDtypeStruct(ml_x.shape, F32),
                     jax.ShapeDtypeStruct((bsz, t, n_gate), F32), jax.ShapeDtypeStruct((bsz, n_gate, t), F32)],
        scratch_shapes=[pltpu.VMEM((tm + 2 * HALO, width), F32)],
        compiler_params=_params("parallel", "parallel"),
        name="mlstm_prep",
    )(ml_x, ml_x, ml_x, *consts)


def _mlstm_body(q_ref, k_ref, v_ref, vt_ref, gcol_ref, grow_ref, c0_ref, n0_ref, m0_ref,
                *rest, reverse, has_add):
    add_ref = rest[0] if has_add else None
    h_ref, cf_ref, nf_ref, mf_ref, c_sc, n_sc, m_sc = rest[1:] if has_add else rest
    i, n = pl.program_id(1), pl.num_programs(1)
    length = q_ref.shape[1]
    n_heads = c_sc.shape[0]
    dh = c_sc.shape[1]

    @pl.when(i == 0)
    def _():
        c_sc[...] = c0_ref[0]
        n_sc[...] = n0_ref[0]
        m_sc[...] = m0_ref[0]

    direction = 1 if reverse else 0
    t_idx = lax.broadcasted_iota(jnp.int32, (length, length), 0)
    s_idx = lax.broadcasted_iota(jnp.int32, (length, length), 1)
    visible = (s_idx >= t_idx) if reverse else (s_idx <= t_idx)
    for h in range(n_heads):
        sl = slice(dh * h, dh * (h + 1))
        ci = n_heads * direction + h
        cb = 2 * n_heads + ci
        qh, kh, vh, vth = q_ref[0, :, sl], k_ref[0, :, sl], v_ref[0, :, sl], vt_ref[0, sl, :]
        ig_row, b_row = grow_ref[0, ci:ci + 1, :], grow_ref[0, cb:cb + 1, :]
        b_col = gcol_ref[0, :, cb:cb + 1]
        g = b_col[0:1] if reverse else b_col[length - 1:length]
        m_st = m_sc[h]
        log_w = jnp.where(visible, b_col - b_row + ig_row, -jnp.inf)
        log_inter = b_col + m_st
        m_t = jnp.maximum(log_inter, jnp.max(log_w, axis=-1, keepdims=True))
        w = jnp.exp(log_w - m_t) * _dot_nt(qh, kh)
        s_inter = jnp.exp(log_inter - m_t)
        num = _dot(w.astype(BF16), vh) + s_inter * _dot_nt(qh, c_sc[h].astype(BF16))
        den = jnp.sum(w, axis=-1, keepdims=True) \
            + s_inter * jnp.sum(qh.astype(F32) * n_sc[h], axis=-1, keepdims=True)
        h_out = num / jnp.maximum(jnp.abs(den), jnp.exp(-m_t))
        h_ref[0, :, sl] = h_out + add_ref[0, :, sl] if has_add else h_out
        log_end = g - b_row + ig_row
        m_new = jnp.maximum(g + m_st, jnp.max(log_end, axis=-1, keepdims=True))
        e_row = jnp.exp(log_end - m_new)
        decay = jnp.exp(g + m_st - m_new)
        c_sc[h] = decay * c_sc[h] + _dot((vth * e_row).astype(BF16), kh)
        e8 = jnp.broadcast_to(e_row, (8, length)).astype(BF16)
        n_sc[h] = decay * n_sc[h] + _dot(e8, kh)[0:1]
        m_sc[h] = m_new

    @pl.when(i == n - 1)
    def _():
        cf_ref[0] = c_sc[...]
        nf_ref[0] = n_sc[...]
        mf_ref[0] = m_sc[...]


def mlstm_direction(q, k, v, vt, gcol, grow, state, reverse, add=None):
    bsz, t, width = q.shape
    extra = () if add is None else (add,)
    n = t // ML_CHUNK
    n_gate = gcol.shape[-1]
    dh = width // ML_HEADS
    tidx = (lambda i: n - 1 - i) if reverse else (lambda i: i)
    row = pl.BlockSpec((1, ML_CHUNK, width), lambda b, i: (b, tidx(i), 0))
    state_specs = [pl.BlockSpec((1,) + s.shape[1:], lambda b, i: (b, 0, 0, 0)) for s in state]
    outs = pl.pallas_call(
        functools.partial(_mlstm_body, reverse=reverse, has_add=add is not None),
        grid=(bsz, n),
        in_specs=[row, row, row, pl.BlockSpec((1, width, ML_CHUNK), lambda b, i: (b, 0, tidx(i))),
                  pl.BlockSpec((1, ML_CHUNK, n_gate), lambda b, i: (b, tidx(i), 0)),
                  pl.BlockSpec((1, n_gate, ML_CHUNK), lambda b, i: (b, 0, tidx(i)))] + state_specs
                 + [row] * len(extra),
        out_specs=[row] + state_specs,
        out_shape=[jax.ShapeDtypeStruct(q.shape, F32)] + [jax.ShapeDtypeStruct(s.shape, F32) for s in state],
        scratch_shapes=[pltpu.VMEM((ML_HEADS, dh, dh), F32), pltpu.VMEM((ML_HEADS, 1, dh), F32),
                        pltpu.VMEM((ML_HEADS, 1, 1), F32)],
        compiler_params=_params("parallel", "arbitrary"),
        name="mlstm_rev" if reverse else "mlstm_fwd",
    )(q, k, v, vt, gcol, grow, *state, *extra)
    return outs[0], tuple(outs[1:])


def _mixout_body(*refs, n_parts, alpha):
    x_ref, mod_ref, ln_ref = refs[:3]
    part_refs, w_refs, o_ref = refs[3:3 + n_parts], refs[3 + n_parts:3 + 2 * n_parts], refs[-1]
    x = x_ref[0]
    y = jnp.zeros(x.shape, F32)
    for p_ref, w_ref in zip(part_refs, w_refs):
        y = y + _dot(p_ref[0].astype(BF16), w_ref[...])
    z = alpha * x + mod_ref[0][2:3] * y
    o_ref[0] = _layer_norm(z, ln_ref[0:1], ln_ref[1:2])


def mixer_out(x, mod, ln, parts, weights, alpha):
    bsz, t, d = x.shape
    tm = _row_tile(t, 512)
    row = lambda w: pl.BlockSpec((1, tm, w), lambda b, i: (b, i, 0))
    return pl.pallas_call(
        functools.partial(_mixout_body, n_parts=len(parts), alpha=alpha),
        grid=(bsz, t // tm),
        in_specs=[row(d), _mod_spec(mod), _const_spec(ln.shape)] + [row(p.shape[-1]) for p in parts]
                 + [_const_spec(w.shape) for w in weights],
        out_specs=row(d),
        out_shape=jax.ShapeDtypeStruct(x.shape, F32),
        compiler_params=_params("parallel", "parallel"),
        name="mixer_out",
    )(x, mod, ln, *parts, *weights)


def _evenout_body(x_ref, mod_ref, ln_ref, rg_ref, gate_ref, ml_ref, mlc_ref, mlz_ref,
                  ng_ref, skip_ref, wrg_ref, wml_ref, o_ref, *, alpha):
    x = x_ref[0]
    y_rg = rg_ref[0] * jax.nn.gelu(gate_ref[0], approximate=True)
    y = _dot(y_rg.astype(BF16), wrg_ref[...])
    h_ml = ml_ref[0]
    z_gate = mlz_ref[0]
    dh = h_ml.shape[1] // ML_HEADS
    for h in range(ML_HEADS):
        sl = slice(dh * h, dh * (h + 1))
        hh = h_ml[:, sl]
        mu = jnp.mean(hh, axis=-1, keepdims=True)
        hc = hh - mu
        var = jnp.mean(hc * hc, axis=-1, keepdims=True)
        hn = hc * lax.rsqrt(var + LN_EPS) * ng_ref[:, sl]
        zg = z_gate[:, sl]
        y_ml = (hn + skip_ref[:, sl] * mlc_ref[0, :, sl]) * (zg * jax.nn.sigmoid(zg))
        y = y + _dot(y_ml.astype(BF16), wml_ref[sl, :])
    z = alpha * x + mod_ref[0][2:3] * y
    o_ref[0] = _layer_norm(z, ln_ref[0:1], ln_ref[1:2])


def even_out(x, mod, ln, streams, norm_g, skip, w_rg, w_ml, alpha):
    bsz, t, d = x.shape
    tm = _row_tile(t, 256)
    row = pl.BlockSpec((1, tm, d), lambda b, i: (b, i, 0))
    vec = _const_spec((1, d))
    return pl.pallas_call(
        functools.partial(_evenout_body, alpha=alpha),
        grid=(bsz, t // tm),
        in_specs=[row, _mod_spec(mod), _const_spec(ln.shape)] + [row] * len(streams)
                 + [vec, vec, _const_spec(w_rg.shape), _const_spec(w_ml.shape)],
        out_specs=row,
        out_shape=jax.ShapeDtypeStruct(x.shape, F32),
        compiler_params=_params("parallel", "parallel"),
        name="even_out",
    )(x, mod, ln, *streams, norm_g, skip, w_rg, w_ml)


def _dense_blocks(w, group):
    g, di, do = w.shape
    wb = w.reshape(g // group, group, di, do)
    eye = jnp.eye(group, dtype=w.dtype)
    return jnp.einsum('cgio,gh->cgiho', wb, eye).reshape(g // group, group * di, group * do).astype(BF16)


def _even_mixer(x_lat, x_ctx, mod_l, mod_c, ln, alpha, p):
    d = x_lat.shape[-1]
    bsz = x_lat.shape[0]
    w_in = p['w_in'].astype(BF16)
    streams = [inproj_even(x_lat, mod_l, w_in), inproj_even(x_ctx, mod_c, w_in)]
    width = streams[0][0].shape[-1]
    dh = width // ML_HEADS
    vec = lambda a: a.reshape(1, -1)
    rg_sum = [None, None]
    for dr, rev in enumerate((False, True)):
        consts = (p['rg_conv_w'], vec(p['rg_conv_b']), _dense_blocks(p['rg_w_a'][dr], MXU_TILE // p['rg_w_a'].shape[-1]),
                  vec(p['rg_b_a'][dr]), _dense_blocks(p['rg_w_x'][dr], MXU_TILE // p['rg_w_x'].shape[-1]),
                  vec(p['rg_b_x'][dr]), vec(p['rg_lambda'][dr]))
        hc, h_end = rglru_direction(streams[1][0], *consts, jnp.zeros((bsz, 1, width), F32), rev, add=rg_sum[1])
        hl, _ = rglru_direction(streams[0][0], *consts, h_end, rev, add=rg_sum[0])
        rg_sum = [hl, hc]
    group = MXU_TILE // p['ml_w_q'].shape[-1]
    wq, wk, wv = (_dense_blocks(p[n], group) for n in ('ml_w_q', 'ml_w_k', 'ml_w_v'))
    wvt = jnp.swapaxes(wv, 1, 2)
    wg_all, bg_all = p['ml_w_gate'], p['ml_b_gate']
    nh = ML_HEADS
    wg = jnp.concatenate([wg_all[0][:, :nh], wg_all[1][:, :nh], wg_all[0][:, nh:], wg_all[1][:, nh:]], axis=1)
    bg = jnp.concatenate([bg_all[0][:nh], bg_all[1][:nh], bg_all[0][nh:], bg_all[1][nh:]])
    tril = jnp.tril(jnp.ones((ML_CHUNK, ML_CHUNK), BF16))
    prep_consts = (p['ml_conv_w'], vec(p['ml_conv_b']), wq, wk, wv, wvt, wg.astype(BF16), wg.T.astype(BF16),
                   bg.reshape(1, -1), bg.reshape(-1, 1), tril, tril.T)
    preps = [mlstm_prep(s[2], *prep_consts, dh ** -0.5) for s in streams]
    ml_sum = [None, None]
    for rev in (False, True):
        state0 = (jnp.zeros((bsz, nh, dh, dh), F32), jnp.zeros((bsz, nh, 1, dh), F32),
                  jnp.zeros((bsz, nh, 1, 1), F32))
        seq = lambda pr: pr[:4] + pr[5:]
        mc, st = mlstm_direction(*seq(preps[1]), state0, rev, add=ml_sum[1])
        ml, _ = mlstm_direction(*seq(preps[0]), st, rev, add=ml_sum[0])
        ml_sum = [ml, mc]
    w_out = p['w_out'].astype(BF16)
    outs = []
    for side, (x_in, mod) in enumerate(((x_lat, mod_l), (x_ctx, mod_c))):
        rg_x, rg_gate, ml_x, ml_z = streams[side]
        outs.append(even_out(x_in, mod, ln, (rg_sum[side], rg_gate, ml_sum[side], preps[side][4], ml_z),
                             vec(p['ml_norm_g']), vec(p['ml_skip']), w_out[:width], w_out[width:], alpha))
    return outs


def kernel(x, c, ctx, c_ctx, w_ada, b_ada, ln_g, ln_b, ffn_w_gate, ffn_w_up, ffn_w_down, ev_w_in, ev_w_out, rg_conv_w, rg_conv_b, rg_w_a, rg_b_a, rg_w_x, rg_b_x, rg_lambda, ml_conv_w, ml_conv_b, ml_w_q, ml_w_k, ml_w_v, ml_w_gate, ml_b_gate, ml_norm_g, ml_skip, od_w_in, od_w_out, cv_conv_w, cv_conv_b, cv_ln_g, cv_ln_b, da_lambda, da_subln_g):
    depth = w_ada.shape[0]
    d = x.shape[-1]
    alpha = (2 * depth) ** 0.25
    bsz, t_len = x.shape[:2]
    ctx_len = ctx.shape[1]
    rope = _rope_tables(t_len)
    x_lat, x_ctx = x, ctx
    cond_lat = jax.nn.silu(c)
    cond_ctx = jax.nn.silu(c_ctx)
    cw = cv_conv_w.shape[-1]
    dw = da_subln_g.shape[-1] * DA_HEADS
    for layer in range(depth):
        need_ctx = layer < depth - 1
        mod_l = (cond_lat @ w_ada[layer] + b_ada[layer]).reshape(bsz, N_SUB, 3, d)
        mod_c = (cond_ctx @ w_ada[layer] + b_ada[layer]).reshape(1, N_SUB, 3, d)
        ln = jnp.stack([ln_g[layer], ln_b[layer]], axis=1)
        ffn1 = _prep_ffn(ffn_w_gate[layer, 0], ffn_w_up[layer, 0], ffn_w_down[layer, 0])
        ffn2 = _prep_ffn(ffn_w_gate[layer, 1], ffn_w_up[layer, 1], ffn_w_down[layer, 1])
        x_lat = ffn_sublayer(x_lat, mod_l[:, 0], *ffn1, ln[0], alpha)
        x_ctx = ffn_sublayer(x_ctx, mod_c[:, 0], *ffn1, ln[0], alpha)
        i = layer // 2
        if layer % 2 == 0:
            p = dict(w_in=ev_w_in[i], w_out=ev_w_out[i], rg_conv_w=rg_conv_w[i], rg_conv_b=rg_conv_b[i],
                     rg_w_a=rg_w_a[i], rg_b_a=rg_b_a[i], rg_w_x=rg_w_x[i], rg_b_x=rg_b_x[i], rg_lambda=rg_lambda[i],
                     ml_conv_w=ml_conv_w[i], ml_conv_b=ml_conv_b[i], ml_w_q=ml_w_q[i], ml_w_k=ml_w_k[i],
                     ml_w_v=ml_w_v[i], ml_w_gate=ml_w_gate[i], ml_b_gate=ml_b_gate[i], ml_norm_g=ml_norm_g[i],
                     ml_skip=ml_skip[i])
            x_lat, x_ctx_new = _even_mixer(x_lat, x_ctx, mod_l[:, 1], mod_c[:, 1], ln[1], alpha, p)
        else:
            w_in = od_w_in[i].astype(BF16)
            w_vt = w_in[:, 2 * cw + 2 * dw:].T
            lam_init = 0.8 - 0.6 * math.exp(-0.3 * layer)
            lamf = da_lambda[i].astype(F32)
            lam = (jnp.exp(jnp.sum(lamf[0] * lamf[1])) - jnp.exp(jnp.sum(lamf[2] * lamf[3])) + lam_init).reshape(1)
            subln = da_subln_g[i].reshape(-1, 1)
            glu_l, q_l, k_l, vt_l = inproj_odd(x_lat, mod_l[:, 1], w_in, w_vt, rope, cw, dw)
            glu_c, q_c, k_c, vt_c = inproj_odd(x_ctx, mod_c[:, 1], w_in, w_vt, None, cw, dw)
            attn_l = diff_attention(q_l, k_c, vt_c, k_l, vt_l, lam, subln, 1.0 - lam_init)
            conv = (cv_conv_w[i], cv_conv_b[i], cv_ln_g[i], cv_ln_b[i])
            w_out = od_w_out[i].astype(BF16)
            w_parts = (w_out[:cw], w_out[cw:])
            x_lat = mixer_out(x_lat, mod_l[:, 1], ln[1], (conformer_conv(glu_l, *conv), attn_l), w_parts, alpha)
            if need_ctx:
                attn_c = diff_attention(q_c, k_c, vt_c, None, None, lam, subln, 1.0 - lam_init)
                x_ctx_new = mixer_out(x_ctx, mod_c[:, 1], ln[1], (conformer_conv(glu_c, *conv), attn_c), w_parts,
                                      alpha)
        x_lat = ffn_sublayer(x_lat, mod_l[:, 2], *ffn2, ln[2], alpha)
        if need_ctx:
            x_ctx = ffn_sublayer(x_ctx_new, mod_c[:, 2], *ffn2, ln[2], alpha)
    return x_lat
```

```python
import functools
import math

import jax
import jax.numpy as jnp
from jax import lax
from jax.experimental import pallas as pl
from jax.experimental.pallas import tpu as pltpu

F32 = jnp.float32
BF16 = jnp.bfloat16

GRID_W = 64
N_SUB = 3
HALF_STEP = 0.5
LN_EPS = 1e-5
RG_C = 8.0
ML_HEADS = 4
DA_HEADS = 8
DA_HEAD_DIM = 64
DA_V_DIM = 2 * DA_HEAD_DIM
ROPE_BASE = 10000.0
LOG2_E = 1.4426950408889634

FFN_CHUNK = 256
MXU_TILE = 256
ML_CHUNK = 256
HALO = 8
CV_HALO = 16
CV_ROWS = 32
SUBLANES = 8
ATTN_TQ = 1024
ATTN_KV = 512
ATTN_SUB = 1
ATTN_ONES = 16
VMEM_LIMIT = 56 * 1024 * 1024


def _params(*sem):
    return pltpu.CompilerParams(dimension_semantics=sem, vmem_limit_bytes=VMEM_LIMIT)


def _row_tile(t, want):
    tm = min(t, want)
    assert t % tm == 0
    return tm


def _layer_norm(z, g, b):
    mu = jnp.mean(z, axis=-1, keepdims=True)
    zc = z - mu
    var = jnp.mean(zc * zc, axis=-1, keepdims=True)
    return zc * lax.rsqrt(var + LN_EPS) * g + b


def _sigmoid(z):
    return 0.5 * jnp.tanh(0.5 * z) + 0.5


def _softplus(z):
    return jnp.maximum(z, 0.0) + jnp.log1p(jnp.exp(-jnp.abs(z)))


def _dot(a, b):
    return jnp.dot(a, b, preferred_element_type=F32)


def _dot_nt(a, b):
    return lax.dot_general(a, b, (((1,), (1,)), ((), ())), preferred_element_type=F32)


def _mod_spec(mod):
    per_sample = mod.shape[0] > 1
    return pl.BlockSpec((1,) + mod.shape[1:], lambda b, i: (b if per_sample else 0, 0, 0))


def _const_spec(shape):
    return pl.BlockSpec(shape, lambda b, i: (0,) * len(shape))


def _halo_specs(t, tm, width, halo, time_index):
    per_tile, n_halo = tm // halo, t // halo
    cur = pl.BlockSpec((1, tm, width), lambda b, i: (b, time_index(i), 0))
    prev = pl.BlockSpec((1, halo, width), lambda b, i: (b, jnp.maximum(time_index(i) * per_tile - 1, 0), 0))
    nxt = pl.BlockSpec((1, halo, width),
                       lambda b, i: (b, jnp.minimum((time_index(i) + 1) * per_tile, n_halo - 1), 0))
    return [cur, prev, nxt]


def _fill_padded(xe_sc, cur, prev_ref, next_ref, ti, n_tiles, halo):
    tm = cur.shape[0]
    xe_sc[0:halo] = prev_ref[0] * (ti > 0).astype(F32)
    xe_sc[halo:halo + tm] = cur
    xe_sc[halo + tm:2 * halo + tm] = next_ref[0] * (ti < n_tiles - 1).astype(F32)


def _short_conv(xe_sc, w_ref, b_ref, tm):
    y = b_ref[...] + w_ref[0:1] * xe_sc[pl.ds(HALO - 1, tm), :]
    for j in range(1, w_ref.shape[0]):
        y = y + w_ref[j:j + 1] * xe_sc[pl.ds(HALO - 1 + j, tm), :]
    return y


def _ffn_body(x_ref, mod_ref, wg_ref, wu_ref, wd_ref, ln_ref, o_ref, *, alpha):
    x = x_ref[0]
    mod = mod_ref[0]
    h = (x * (1.0 + mod[1:2]) + mod[0:1]).astype(BF16)
    y = jnp.zeros(x.shape, F32)
    for c in range(0, wg_ref.shape[1], FFN_CHUNK):
        g = _dot(h, wg_ref[:, c:c + FFN_CHUNK])
        u = _dot(h, wu_ref[:, c:c + FFN_CHUNK])
        a = (g * jax.nn.sigmoid(g) * u).astype(BF16)
        y = y + _dot(a, wd_ref[c:c + FFN_CHUNK, :])
    z = alpha * x + (HALF_STEP * mod[2:3]) * y
    o_ref[0] = _layer_norm(z, ln_ref[0:1], ln_ref[1:2])


def ffn_sublayer(x, mod, wg, wu, wd, ln, alpha):
    bsz, t, d = x.shape
    tm = _row_tile(t, 512)
    return pl.pallas_call(
        functools.partial(_ffn_body, alpha=alpha),
        grid=(bsz, t // tm),
        in_specs=[pl.BlockSpec((1, tm, d), lambda b, i: (b, i, 0)), _mod_spec(mod),
                  _const_spec(wg.shape), _const_spec(wu.shape), _const_spec(wd.shape), _const_spec(ln.shape)],
        out_specs=pl.BlockSpec((1, tm, d), lambda b, i: (b, i, 0)),
        out_shape=jax.ShapeDtypeStruct(x.shape, F32),
        compiler_params=_params("parallel", "parallel"),
        name="ffn_sublayer",
    )(x, mod, wg, wu, wd, ln)


def _prep_ffn(wg, wu, wd):
    assert wg.shape[1] % FFN_CHUNK == 0
    return wg.astype(BF16), wu.astype(BF16), wd.astype(BF16)


def _inproj_even_body(x_ref, mod_ref, w_ref, *o_refs):
    mod = mod_ref[0]
    h = (x_ref[0] * (1.0 + mod[1:2]) + mod[0:1]).astype(BF16)
    width = o_refs[0].shape[-1]
    for j, o_ref in enumerate(o_refs):
        o_ref[0] = _dot(h, w_ref[:, j * width:(j + 1) * width])


def inproj_even(x, mod, w_in):
    bsz, t, d = x.shape
    tm = _row_tile(t, 512)
    n_out = w_in.shape[1] // d
    row_spec = pl.BlockSpec((1, tm, d), lambda b, i: (b, i, 0))
    return pl.pallas_call(
        _inproj_even_body,
        grid=(bsz, t // tm),
        in_specs=[row_spec, _mod_spec(mod), _const_spec(w_in.shape)],
        out_specs=[row_spec] * n_out,
        out_shape=[jax.ShapeDtypeStruct(x.shape, F32)] * n_out,
        compiler_params=_params("parallel", "parallel"),
        name="inproj_even",
    )(x, mod, w_in)


def _rope128(a, cos, sin_lo, sin_hi):
    return a * cos + pltpu.roll(a, 112, 1) * sin_lo + pltpu.roll(a, 16, 1) * sin_hi


def _inproj_odd_body(x_ref, mod_ref, w_ref, wvt_ref, *rest, use_rope):
    if use_rope:
        cos_ref, slo_ref, shi_ref, glu_ref, q_ref, k_ref, vt_ref = rest
    else:
        glu_ref, q_ref, k_ref, vt_ref = rest
    mod = mod_ref[0]
    h = (x_ref[0] * (1.0 + mod[1:2]) + mod[0:1]).astype(BF16)
    cw = glu_ref.shape[-1]
    a = _dot(h, w_ref[:, :cw])
    gate = _dot(h, w_ref[:, cw:2 * cw])
    glu_ref[0] = a * jax.nn.sigmoid(gate)
    dw = q_ref.shape[-1]
    q = _dot(h, w_ref[:, 2 * cw:2 * cw + dw]) * (DA_HEAD_DIM ** -0.5 * LOG2_E)
    k = _dot(h, w_ref[:, 2 * cw + dw:2 * cw + 2 * dw])
    if use_rope:
        cos, slo, shi = cos_ref[...], slo_ref[...], shi_ref[...]
        for j in range(dw // 128):
            sl = slice(128 * j, 128 * (j + 1))
            q_ref[0, :, sl] = _rope128(q[:, sl], cos, slo, shi).astype(BF16)
            k_ref[0, :, sl] = _rope128(k[:, sl], cos, slo, shi).astype(BF16)
    else:
        q_ref[0] = q.astype(BF16)
        k_ref[0] = k.astype(BF16)
    vt_ref[0, 0] = _dot_nt(wvt_ref[...], h).astype(BF16)


def inproj_odd(x, mod, w_in, w_vt, rope, cw, dw):
    bsz, t, d = x.shape
    tm = _row_tile(t, ATTN_KV)
    row = lambda w: pl.BlockSpec((1, tm, w), lambda b, i: (b, i, 0))
    in_specs = [row(d), _mod_spec(mod), _const_spec(w_in.shape), _const_spec(w_vt.shape)]
    args = [x, mod, w_in, w_vt]
    if rope is not None:
        in_specs += [pl.BlockSpec((tm, 128), lambda b, i: (i, 0))] * 3
        args += list(rope)
    return pl.pallas_call(
        functools.partial(_inproj_odd_body, use_rope=rope is not None),
        grid=(bsz, t // tm),
        in_specs=in_specs,
        out_specs=[row(cw), row(dw), row(dw), pl.BlockSpec((1, 1, dw, tm), lambda b, i: (b, i, 0, 0))],
        out_shape=[jax.ShapeDtypeStruct((bsz, t, cw), F32)] + [jax.ShapeDtypeStruct((bsz, t, dw), BF16)] * 2
                  + [jax.ShapeDtypeStruct((bsz, t // tm, dw, tm), BF16)],
        compiler_params=_params("parallel", "parallel"),
        name="inproj_odd",
    )(*args)


def _rope_tables(t):
    pos = jnp.arange(t)
    row = (pos // GRID_W).astype(F32)
    col = (pos % GRID_W).astype(F32)
    axis_dim = DA_HEAD_DIM // 2
    inv_freq = ROPE_BASE ** (-jnp.arange(0, axis_dim, 2, dtype=F32) / axis_dim)
    ang_r = row[:, None] * inv_freq
    ang_c = col[:, None] * inv_freq
    zeros = jnp.zeros_like(ang_r)
    cos64 = jnp.concatenate([jnp.cos(ang_r)] * 2 + [jnp.cos(ang_c)] * 2, axis=-1)
    sin_lo64 = jnp.concatenate([-jnp.sin(ang_r), zeros, -jnp.sin(ang_c), zeros], axis=-1)
    sin_hi64 = jnp.concatenate([zeros, jnp.sin(ang_r), zeros, jnp.sin(ang_c)], axis=-1)
    return tuple(jnp.tile(a, (1, 2)) for a in (cos64, sin_lo64, sin_hi64))


def _attn_body(lam_ref, q_ref, kc_ref, vtc_ref, *rest, out_scale, has_lat):
    if has_lat:
        kl_ref, vtl_ref, g_ref, o_ref, m_sc, acc_sc, pa_sc, ca_sc, sa_sc, sb_sc, pb_sc, cb_sc = rest
    else:
        g_ref, o_ref, m_sc, acc_sc, pa_sc, ca_sc = rest
    q = q_ref[0]
    lane = lax.broadcasted_iota(jnp.int32, q.shape, 1)
    zero = jnp.zeros_like(q)
    q_maps = (jnp.where(lane < DA_HEAD_DIM, q, zero), jnp.where(lane >= DA_HEAD_DIM, q, zero))
    m_sc[...] = jnp.full(m_sc.shape, -jnp.inf, F32)
    acc_sc[...] = jnp.zeros(acc_sc.shape, F32)

    def scores(k):
        return [_dot_nt(k, q_maps[m]) for m in range(2)]

    def softmax(st_maps, p_sc, c_sc):
        tk = st_maps[0].shape[0]
        for m in range(2):
            m_prev = m_sc[m]
            m_new = jnp.maximum(m_prev, jnp.max(st_maps[m], axis=0, keepdims=True))
            p_sc[m, 0:tk] = jnp.exp2(st_maps[m] - m_new).astype(BF16)
            c_sc[m] = jnp.exp2(m_prev - m_new)
            m_sc[m] = m_new

    def accumulate(vt, p_sc, c_sc):
        tk = vt.shape[1]
        vt_ext = jnp.concatenate([vt, jnp.ones((ATTN_ONES, tk), BF16)], axis=0)
        for m in range(2):
            acc_sc[m] = c_sc[m] * acc_sc[m] + _dot(vt_ext, p_sc[m, 0:tk])

    if not has_lat:
        softmax(scores(kc_ref[0]), pa_sc, ca_sc)
        accumulate(vtc_ref[0, 0], pa_sc, ca_sc)
    else:
        block = sa_sc.shape[1]
        sub = block // ATTN_KV
        n_blocks = vtl_ref.shape[1] // sub

        def k_block(j):
            return kl_ref[0, pl.ds(pl.multiple_of(j * block, block), block), :]

        def vt_block(j):
            return jnp.concatenate([vtl_ref[0, j * sub + s] for s in range(sub)], axis=1)

        def put(dst_sc, st_maps):
            dst_sc[0], dst_sc[1] = st_maps

        n_ctx = kc_ref.shape[1]
        softmax(scores(kc_ref[0]), pb_sc, cb_sc)
        pb_sc[:, n_ctx:, :] = jnp.zeros((2, block - n_ctx, pb_sc.shape[2]), BF16)
        vt_ctx = jnp.concatenate([vtc_ref[0, 0], jnp.zeros((DA_V_DIM, block - n_ctx), BF16)], axis=1)
        put(sa_sc, scores(k_block(0)))

        def step(jj, carry):
            j = 2 * jj
            put(sb_sc, scores(k_block(j + 1)))
            vt_prev = jnp.where(jj == 0, vt_ctx, vt_block(jnp.maximum(j - 1, 0)))
            accumulate(vt_prev, pb_sc, cb_sc)
            softmax((sa_sc[0], sa_sc[1]), pa_sc, ca_sc)
            put(sa_sc, scores(k_block(jnp.minimum(j + 2, n_blocks - 1))))
            accumulate(vt_block(j), pa_sc, ca_sc)
            softmax((sb_sc[0], sb_sc[1]), pb_sc, cb_sc)
            return carry

        lax.fori_loop(0, n_blocks // 2, step, 0)
        accumulate(vt_block(n_blocks - 1), pb_sc, cb_sc)

    o0 = acc_sc[0, :DA_V_DIM] / acc_sc[0, DA_V_DIM:DA_V_DIM + 1]
    o1 = acc_sc[1, :DA_V_DIM] / acc_sc[1, DA_V_DIM:DA_V_DIM + 1]
    o = o0 - lam_ref[0] * o1
    o = o * lax.rsqrt(jnp.mean(o * o, axis=0, keepdims=True) + LN_EPS) * (g_ref[...] * out_scale)
    o_ref[0] = o.T


def diff_attention(q, k_ctx, vt_ctx, k_lat, vt_lat, lam, subln_g, out_scale):
    bsz, t, dw = q.shape
    n_heads = dw // DA_V_DIM
    tq = _row_tile(t, ATTN_TQ)
    has_lat = k_lat is not None
    kv_specs = [pl.BlockSpec((1, k_ctx.shape[1], DA_V_DIM), lambda b, h, i: (b, 0, h)),
                pl.BlockSpec((1, 1, DA_V_DIM, vt_ctx.shape[-1]), lambda b, h, i: (b, 0, h, 0))]
    kv_args = [k_ctx, vt_ctx]
    if has_lat:
        assert vt_lat.shape[-1] == ATTN_KV
        kv_specs += [pl.BlockSpec((1, k_lat.shape[1], DA_V_DIM), lambda b, h, i: (b, 0, h)),
                     pl.BlockSpec((1, vt_lat.shape[1], DA_V_DIM, ATTN_KV), lambda b, h, i: (b, 0, h, 0))]
        kv_args += [k_lat, vt_lat]
        sub = ATTN_SUB if vt_lat.shape[1] % (2 * ATTN_SUB) == 0 else 1
        assert vt_lat.shape[1] % (2 * sub) == 0
        assert k_ctx.shape[1] < sub * ATTN_KV
    tk_max = sub * ATTN_KV if has_lat else k_ctx.shape[1]
    assert k_ctx.shape[1] <= tk_max
    weights = [pltpu.VMEM((2, tk_max, tq), BF16), pltpu.VMEM((2, 1, tq), F32)]
    scratch = [pltpu.VMEM((2, 1, tq), F32), pltpu.VMEM((2, DA_V_DIM + ATTN_ONES, tq), F32)] + weights
    if has_lat:
        scratch += [pltpu.VMEM((2, tk_max, tq), F32)] * 2 + weights
    return pl.pallas_call(
        functools.partial(_attn_body, out_scale=out_scale, has_lat=has_lat),
        grid=(bsz, n_heads, t // tq),
        in_specs=[pl.BlockSpec(memory_space=pltpu.SMEM),
                  pl.BlockSpec((1, tq, DA_V_DIM), lambda b, h, i: (b, i, h))] + kv_specs
                 + [pl.BlockSpec((DA_V_DIM, 1), lambda b, h, i: (0, 0))],
        out_specs=pl.BlockSpec((1, tq, DA_V_DIM), lambda b, h, i: (b, i, h)),
        out_shape=jax.ShapeDtypeStruct((bsz, t, dw), F32),
        scratch_shapes=scratch,
        compiler_params=_params("parallel", "parallel", "parallel"),
        name="diff_attention",
    )(lam, q, *kv_args, subln_g)


def _cvconv_body(x_ref, prev_ref, next_ref, w_ref, b_ref, g_ref, beta_ref, o_ref, xe_sc, xs_sc):
    i, n = pl.program_id(1), pl.num_programs(1)
    tm = x_ref.shape[1]
    _fill_padded(xe_sc, x_ref[0], prev_ref, next_ref, i, n, CV_HALO)
    n_taps = w_ref.shape[0]
    first = CV_HALO - (n_taps - 1) // 2
    rows = xs_sc.shape[1]
    for r in range(SUBLANES):
        xs_sc[r] = xe_sc[r:r + rows, :]

    for base in range(0, tm, CV_ROWS):
        y = b_ref[...]
        for j in range(n_taps):
            phase, start = (first + j) % SUBLANES, base + (first + j) // SUBLANES * SUBLANES
            y = y + w_ref[j:j + 1] * xs_sc[phase, start:start + CV_ROWS, :]
        y = _layer_norm(y, g_ref[...], beta_ref[...])
        o_ref[0, base:base + CV_ROWS, :] = y * _sigmoid(y)


def conformer_conv(glu, conv_w, conv_b, ln_g, ln_b):
    bsz, t, cw = glu.shape
    tm = _row_tile(t, 512)
    vec = lambda a: a.reshape(1, cw)
    return pl.pallas_call(
        _cvconv_body,
        grid=(bsz, t // tm),
        in_specs=_halo_specs(t, tm, cw, CV_HALO, lambda i: i)
                 + [_const_spec(conv_w.shape)] + [_const_spec((1, cw))] * 3,
        out_specs=pl.BlockSpec((1, tm, cw), lambda b, i: (b, i, 0)),
        out_shape=jax.ShapeDtypeStruct(glu.shape, F32),
        scratch_shapes=[pltpu.VMEM((tm + 2 * CV_HALO, cw), F32),
                        pltpu.VMEM((SUBLANES, tm + 2 * CV_HALO - SUBLANES, cw), F32)],
        compiler_params=_params("parallel", "parallel"),
        name="conformer_conv",
    )(glu, glu, glu, conv_w, vec(conv_b), vec(ln_g), vec(ln_b))


def _rglru_body(x_ref, prev_ref, next_ref, cw_ref, cb_ref, wa_ref, ba_ref, wx_ref, bx_ref, lam_ref, h0_ref,
                *rest, reverse, has_add):
    add_ref = rest[0] if has_add else None
    h_ref, hlast_ref, xe_sc, a_sc, b_sc, state_sc = rest[1:] if has_add else rest
    i, n = pl.program_id(1), pl.num_programs(1)
    ti = n - 1 - i if reverse else i
    tm = x_ref.shape[1]

    @pl.when(i == 0)
    def _():
        state_sc[...] = h0_ref[0]

    _fill_padded(xe_sc, x_ref[0], prev_ref, next_ref, ti, n, HALO)
    u = _short_conv(xe_sc, cw_ref, cb_ref, tm)
    ub = u.astype(BF16)
    for c in range(wa_ref.shape[0]):
        sl = slice(MXU_TILE * c, MXU_TILE * (c + 1))
        r = _sigmoid(_dot(ub[:, sl], wa_ref[c]) + ba_ref[:, sl])
        gate_in = _sigmoid(_dot(ub[:, sl], wx_ref[c]) + bx_ref[:, sl])
        log_a = (-RG_C) * r * _softplus(-lam_ref[:, sl])
        a = jnp.exp(log_a)
        a_sc[:, sl] = a
        b_sc[:, sl] = jnp.sqrt(-jnp.tanh(log_a) * (a * a + 1.0)) * gate_in * u[:, sl]

    def step(s, h):
        row = tm - 1 - s if reverse else s
        h = a_sc[pl.ds(row, 1), :] * h + b_sc[pl.ds(row, 1), :]
        h_ref[0, pl.ds(row, 1), :] = h
        return h

    h = lax.fori_loop(0, tm, step, state_sc[...], unroll=8)
    state_sc[...] = h
    if has_add:
        h_ref[0] = h_ref[0] + add_ref[0]

    @pl.when(i == n - 1)
    def _():
        hlast_ref[0] = h


def rglru_direction(rg_x, conv_w, conv_b, w_a, b_a, w_x, b_x, lam, h0, reverse, add=None):
    bsz, t, width = rg_x.shape
    tm = _row_tile(t, 512)
    n = t // tm
    tidx = (lambda i: n - 1 - i) if reverse else (lambda i: i)
    vec = _const_spec((1, width))
    state_spec = pl.BlockSpec((1, 1, width), lambda b, i: (b, 0, 0))
    tile_spec = pl.BlockSpec((1, tm, width), lambda b, i: (b, tidx(i), 0))
    extra = () if add is None else (add,)
    return pl.pallas_call(
        functools.partial(_rglru_body, reverse=reverse, has_add=add is not None),
        grid=(bsz, n),
        in_specs=_halo_specs(t, tm, width, HALO, tidx)
                 + [_const_spec(conv_w.shape), vec, _const_spec(w_a.shape), vec, _const_spec(w_x.shape), vec, vec,
                    state_spec] + [tile_spec] * len(extra),
        out_specs=[tile_spec, state_spec],
        out_shape=[jax.ShapeDtypeStruct(rg_x.shape, F32), jax.ShapeDtypeStruct((bsz, 1, width), F32)],
        scratch_shapes=[pltpu.VMEM((tm + 2 * HALO, width), F32), pltpu.VMEM((tm, width), F32),
                        pltpu.VMEM((tm, width), F32), pltpu.VMEM((1, width), F32)],
        compiler_params=_params("parallel", "arbitrary"),
        name="rglru_rev" if reverse else "rglru_fwd",
    )(rg_x, rg_x, rg_x, conv_w, conv_b, w_a, b_a, w_x, b_x, lam, h0, *extra)


def _split3(a):
    hi = a.astype(BF16)
    rest = a - hi.astype(F32)
    mid = rest.astype(BF16)
    return hi, mid, (rest - mid.astype(F32)).astype(BF16)


def _mlprep_body(x_ref, prev_ref, next_ref, cw_ref, cb_ref, wq_ref, wk_ref, wv_ref, wvt_ref, wg_ref, wgt_ref,
                 bg_ref, bgt_ref, tril_ref, triu_ref,
                 q_ref, k_ref, v_ref, vt_ref, mlc_ref, gcol_ref, grow_ref, xe_sc, *, k_scale):
    i, n = pl.program_id(1), pl.num_programs(1)
    tm = x_ref.shape[1]
    x = x_ref[0]
    _fill_padded(xe_sc, x, prev_ref, next_ref, i, n, HALO)
    conv = _short_conv(xe_sc, cw_ref, cb_ref, tm)
    ml_c = conv * _sigmoid(conv)
    mlc_ref[0] = ml_c
    cb16, xb16 = ml_c.astype(BF16), x.astype(BF16)
    width = x.shape[1]
    n_gate = wg_ref.shape[1]
    pre = jnp.zeros((tm, n_gate), F32) + bg_ref[...]
    pre_t = jnp.zeros((n_gate, tm), F32) + bgt_ref[...]
    for c in range(wq_ref.shape[0]):
        sl = slice(MXU_TILE * c, MXU_TILE * (c + 1))
        q = _dot(cb16[:, sl], wq_ref[c]).astype(BF16)
        k = _dot(cb16[:, sl], wk_ref[c]).astype(BF16)
        v = _dot(xb16[:, sl], wv_ref[c]).astype(BF16)
        q_ref[0, :, sl] = q
        k_ref[0, :, sl] = k * k_scale
        v_ref[0, :, sl] = v
        vt_ref[0, sl, :] = _dot_nt(wvt_ref[c], xb16[:, sl]).astype(BF16)
        for part, val in enumerate((q, k, v)):
            rows = slice(part * width + MXU_TILE * c, part * width + MXU_TILE * (c + 1))
            pre = pre + _dot(val, wg_ref[rows, :])
            pre_t = pre_t + _dot_nt(wgt_ref[:, rows], val)
    half = n_gate // 2
    col = lax.broadcasted_iota(jnp.int32, (ML_CHUNK, n_gate), 1)
    row = lax.broadcasted_iota(jnp.int32, (n_gate, ML_CHUNK), 0)
    for c in range(tm // ML_CHUNK):
        ch = slice(ML_CHUNK * c, ML_CHUNK * (c + 1))
        pre_c, pre_tc = pre[ch], pre_t[:, ch]
        logf_c = jnp.where(col >= half, -_softplus(-pre_c), 0.0)
        logf_tc = jnp.where(row >= half, -_softplus(-pre_tc), 0.0)
        pieces, pieces_t = _split3(logf_c), _split3(logf_tc)
        prefix = sum(_dot(tril_ref[...], p) for p in pieces)
        suffix = sum(_dot(triu_ref[...], p) for p in pieces)
        prefix_t = sum(_dot(p, triu_ref[...]) for p in pieces_t)
        suffix_t = sum(_dot(p, tril_ref[...]) for p in pieces_t)
        gcol_ref[0, ch, :] = jnp.where(col < half, pre_c, jnp.where(col < half + half // 2, prefix, suffix))
        grow_ref[0, :, ch] = jnp.where(row < half, pre_tc, jnp.where(row < half + half // 2, prefix_t, suffix_t))


def mlstm_prep(ml_x, conv_w, conv_b, wq, wk, wv, wvt, wg, wgt, bg, bgt, tril, triu, k_scale):
    bsz, t, width = ml_x.shape
    tm = _row_tile(t, 512)
    n_gate = wg.shape[1]
    row = pl.BlockSpec((1, tm, width), lambda b, i: (b, i, 0))
    consts = (conv_w, conv_b, wq, wk, wv, wvt, wg, wgt, bg, bgt, tril, triu)
    return pl.pallas_call(
        functools.partial(_mlprep_body, k_scale=k_scale),
        grid=(bsz, t // tm),
        in_specs=_halo_specs(t, tm, width, HALO, lambda i: i) + [_const_spec(a.shape) for a in consts],
        out_specs=[row, row, row, pl.BlockSpec((1, width, tm), lambda b, i: (b, 0, i)), row,
                   pl.BlockSpec((1, tm, n_gate), lambda b, i: (b, i, 0)),
                   pl.BlockSpec((1, n_gate, tm), lambda b, i: (b, 0, i))],
        out_shape=[jax.ShapeDtypeStruct(ml_x.shape, BF16)] * 3
                  + [jax.ShapeDtypeStruct((bsz, width, t), BF16), jax.ShapeDtypeStruct(ml_x.shape, F32),
                     jax.ShapeDtypeStruct((bsz, t, n_gate), F32), jax.ShapeDtypeStruct((bsz, n_gate, t), F32)],
        scratch_shapes=[pltpu.VMEM((tm + 2 * HALO, width), F32)],
        compiler_params=_params("parallel", "parallel"),
        name="mlstm_prep",
    )(ml_x, ml_x, ml_x, *consts)


def _mlstm_body(q_ref, k_ref, v_ref, vt_ref, gcol_ref, grow_ref, c0_ref, n0_ref, m0_ref,
                *rest, reverse, has_add):
    add_ref = rest[0] if has_add else None
    h_ref, cf_ref, nf_ref, mf_ref, c_sc, n_sc, m_sc = rest[1:] if has_add else rest
    i, n = pl.program_id(1), pl.num_programs(1)
    length = q_ref.shape[1]
    n_heads = c_sc.shape[0]
    dh = c_sc.shape[1]

    @pl.when(i == 0)
    def _():
        c_sc[...] = c0_ref[0]
        n_sc[...] = n0_ref[0]
        m_sc[...] = m0_ref[0]

    direction = 1 if reverse else 0
    t_idx = lax.broadcasted_iota(jnp.int32, (length, length), 0)
    s_idx = lax.broadcasted_iota(jnp.int32, (length, length), 1)
    visible = (s_idx >= t_idx) if reverse else (s_idx <= t_idx)
    for h in range(n_heads):
        sl = slice(dh * h, dh * (h + 1))
        ci = n_heads * direction + h
        cb = 2 * n_heads + ci
        qh, kh, vh, vth = q_ref[0, :, sl], k_ref[0, :, sl], v_ref[0, :, sl], vt_ref[0, sl, :]
        ig_row, b_row = grow_ref[0, ci:ci + 1, :], grow_ref[0, cb:cb + 1, :]
        b_col = gcol_ref[0, :, cb:cb + 1]
        g = b_col[0:1] if reverse else b_col[length - 1:length]
        m_st = m_sc[h]
        log_w = jnp.where(visible, b_col - b_row + ig_row, -jnp.inf)
        log_inter = b_col + m_st
        m_t = jnp.maximum(log_inter, jnp.max(log_w, axis=-1, keepdims=True))
        w = jnp.exp(log_w - m_t) * _dot_nt(qh, kh)
        s_inter = jnp.exp(log_inter - m_t)
        num = _dot(w.astype(BF16), vh) + s_inter * _dot_nt(qh, c_sc[h].astype(BF16))
        den = jnp.sum(w, axis=-1, keepdims=True) \
            + s_inter * jnp.sum(qh.astype(F32) * n_sc[h], axis=-1, keepdims=True)
        h_out = num / jnp.maximum(jnp.abs(den), jnp.exp(-m_t))
        h_ref[0, :, sl] = h_out + add_ref[0, :, sl] if has_add else h_out
        log_end = g - b_row + ig_row
        m_new = jnp.maximum(g + m_st, jnp.max(log_end, axis=-1, keepdims=True))
        e_row = jnp.exp(log_end - m_new)
        decay = jnp.exp(g + m_st - m_new)
        c_sc[h] = decay * c_sc[h] + _dot((vth * e_row).astype(BF16), kh)
        e8 = jnp.broadcast_to(e_row, (8, length)).astype(BF16)
        n_sc[h] = decay * n_sc[h] + _dot(e8, kh)[0:1]
        m_sc[h] = m_new

    @pl.when(i == n - 1)
    def _():
        cf_ref[0] = c_sc[...]
        nf_ref[0] = n_sc[...]
        mf_ref[0] = m_sc[...]


def mlstm_direction(q, k, v, vt, gcol, grow, state, reverse, add=None):
    bsz, t, width = q.shape
    extra = () if add is None else (add,)
    n = t // ML_CHUNK
    n_gate = gcol.shape[-1]
    dh = width // ML_HEADS
    tidx = (lambda i: n - 1 - i) if reverse else (lambda i: i)
    row = pl.BlockSpec((1, ML_CHUNK, width), lambda b, i: (b, tidx(i), 0))
    state_specs = [pl.BlockSpec((1,) + s.shape[1:], lambda b, i: (b, 0, 0, 0)) for s in state]
    outs = pl.pallas_call(
        functools.partial(_mlstm_body, reverse=reverse, has_add=add is not None),
        grid=(bsz, n),
        in_specs=[row, row, row, pl.BlockSpec((1, width, ML_CHUNK), lambda b, i: (b, 0, tidx(i))),
                  pl.BlockSpec((1, ML_CHUNK, n_gate), lambda b, i: (b, tidx(i), 0)),
                  pl.BlockSpec((1, n_gate, ML_CHUNK), lambda b, i: (b, 0, tidx(i)))] + state_specs
                 + [row] * len(extra),
        out_specs=[row] + state_specs,
        out_shape=[jax.ShapeDtypeStruct(q.shape, F32)] + [jax.ShapeDtypeStruct(s.shape, F32) for s in state],
        scratch_shapes=[pltpu.VMEM((ML_HEADS, dh, dh), F32), pltpu.VMEM((ML_HEADS, 1, dh), F32),
                        pltpu.VMEM((ML_HEADS, 1, 1), F32)],
        compiler_params=_params("parallel", "arbitrary"),
        name="mlstm_rev" if reverse else "mlstm_fwd",
    )(q, k, v, vt, gcol, grow, *state, *extra)
    return outs[0], tuple(outs[1:])


def _mixout_body(*refs, n_parts, alpha):
    x_ref, mod_ref, ln_ref = refs[:3]
    part_refs, w_refs, o_ref = refs[3:3 + n_parts], refs[3 + n_parts:3 + 2 * n_parts], refs[-1]
    x = x_ref[0]
    y = jnp.zeros(x.shape, F32)
    for p_ref, w_ref in zip(part_refs, w_refs):
        y = y + _dot(p_ref[0].astype(BF16), w_ref[...])
    z = alpha * x + mod_ref[0][2:3] * y
    o_ref[0] = _layer_norm(z, ln_ref[0:1], ln_ref[1:2])


def mixer_out(x, mod, ln, parts, weights, alpha):
    bsz, t, d = x.shape
    tm = _row_tile(t, 512)
    row = lambda w: pl.BlockSpec((1, tm, w), lambda b, i: (b, i, 0))
    return pl.pallas_call(
        functools.partial(_mixout_body, n_parts=len(parts), alpha=alpha),
        grid=(bsz, t // tm),
        in_specs=[row(d), _mod_spec(mod), _const_spec(ln.shape)] + [row(p.shape[-1]) for p in parts]
                 + [_const_spec(w.shape) for w in weights],
        out_specs=row(d),
        out_shape=jax.ShapeDtypeStruct(x.shape, F32),
        compiler_params=_params("parallel", "parallel"),
        name="mixer_out",
    )(x, mod, ln, *parts, *weights)


def _evenout_body(x_ref, mod_ref, ln_ref, rg_ref, gate_ref, ml_ref, mlc_ref, mlz_ref,
                  ng_ref, skip_ref, wrg_ref, wml_ref, o_ref, *, alpha):
    x = x_ref[0]
    y_rg = rg_ref[0] * jax.nn.gelu(gate_ref[0], approximate=True)
    y = _dot(y_rg.astype(BF16), wrg_ref[...])
    h_ml = ml_ref[0]
    z_gate = mlz_ref[0]
    dh = h_ml.shape[1] // ML_HEADS
    for h in range(ML_HEADS):
        sl = slice(dh * h, dh * (h + 1))
        hh = h_ml[:, sl]
        mu = jnp.mean(hh, axis=-1, keepdims=True)
        hc = hh - mu
        var = jnp.mean(hc * hc, axis=-1, keepdims=True)
        hn = hc * lax.rsqrt(var + LN_EPS) * ng_ref[:, sl]
        zg = z_gate[:, sl]
        y_ml = (hn + skip_ref[:, sl] * mlc_ref[0, :, sl]) * (zg * _sigmoid(zg))
        y = y + _dot(y_ml.astype(BF16), wml_ref[sl, :])
    z = alpha * x + mod_ref[0][2:3] * y
    o_ref[0] = _layer_norm(z, ln_ref[0:1], ln_ref[1:2])


def even_out(x, mod, ln, streams, norm_g, skip, w_rg, w_ml, alpha):
    bsz, t, d = x.shape
    tm = _row_tile(t, 256)
    row = pl.BlockSpec((1, tm, d), lambda b, i: (b, i, 0))
    vec = _const_spec((1, d))
    return pl.pallas_call(
        functools.partial(_evenout_body, alpha=alpha),
        grid=(bsz, t // tm),
        in_specs=[row, _mod_spec(mod), _const_spec(ln.shape)] + [row] * len(streams)
                 + [vec, vec, _const_spec(w_rg.shape), _const_spec(w_ml.shape)],
        out_specs=row,
        out_shape=jax.ShapeDtypeStruct(x.shape, F32),
        compiler_params=_params("parallel", "parallel"),
        name="even_out",
    )(x, mod, ln, *streams, norm_g, skip, w_rg, w_ml)


def _dense_blocks(w, group):
    g, di, do = w.shape
    wb = w.reshape(g // group, group, di, do)
    eye = jnp.eye(group, dtype=w.dtype)
    return jnp.einsum('cgio,gh->cgiho', wb, eye).reshape(g // group, group * di, group * do).astype(BF16)


def _even_mixer(x_lat, x_ctx, mod_l, mod_c, ln, alpha, p):
    d = x_lat.shape[-1]
    bsz = x_lat.shape[0]
    w_in = p['w_in'].astype(BF16)
    streams = [inproj_even(x_lat, mod_l, w_in), inproj_even(x_ctx, mod_c, w_in)]
    width = streams[0][0].shape[-1]
    dh = width // ML_HEADS
    vec = lambda a: a.reshape(1, -1)
    rg_sum = [None, None]
    for dr, rev in enumerate((False, True)):
        consts = (p['rg_conv_w'], vec(p['rg_conv_b']), _dense_blocks(p['rg_w_a'][dr], MXU_TILE // p['rg_w_a'].shape[-1]),
                  vec(p['rg_b_a'][dr]), _dense_blocks(p['rg_w_x'][dr], MXU_TILE // p['rg_w_x'].shape[-1]),
                  vec(p['rg_b_x'][dr]), vec(p['rg_lambda'][dr]))
        hc, h_end = rglru_direction(streams[1][0], *consts, jnp.zeros((bsz, 1, width), F32), rev, add=rg_sum[1])
        hl, _ = rglru_direction(streams[0][0], *consts, h_end, rev, add=rg_sum[0])
        rg_sum = [hl, hc]
    group = MXU_TILE // p['ml_w_q'].shape[-1]
    wq, wk, wv = (_dense_blocks(p[n], group) for n in ('ml_w_q', 'ml_w_k', 'ml_w_v'))
    wvt = jnp.swapaxes(wv, 1, 2)
    wg_all, bg_all = p['ml_w_gate'], p['ml_b_gate']
    nh = ML_HEADS
    wg = jnp.concatenate([wg_all[0][:, :nh], wg_all[1][:, :nh], wg_all[0][:, nh:], wg_all[1][:, nh:]], axis=1)
    bg = jnp.concatenate([bg_all[0][:nh], bg_all[1][:nh], bg_all[0][nh:], bg_all[1][nh:]])
    tril = jnp.tril(jnp.ones((ML_CHUNK, ML_CHUNK), BF16))
    prep_consts = (p['ml_conv_w'], vec(p['ml_conv_b']), wq, wk, wv, wvt, wg.astype(BF16), wg.T.astype(BF16),
                   bg.reshape(1, -1), bg.reshape(-1, 1), tril, tril.T)
    preps = [mlstm_prep(s[2], *prep_consts, dh ** -0.5) for s in streams]
    ml_sum = [None, None]
    for rev in (False, True):
        state0 = (jnp.zeros((bsz, nh, dh, dh), F32), jnp.zeros((bsz, nh, 1, dh), F32),
                  jnp.zeros((bsz, nh, 1, 1), F32))
        seq = lambda pr: pr[:4] + pr[5:]
        mc, st = mlstm_direction(*seq(preps[1]), state0, rev, add=ml_sum[1])
        ml, _ = mlstm_direction(*seq(preps[0]), st, rev, add=ml_sum[0])
        ml_sum = [ml, mc]
    w_out = p['w_out'].astype(BF16)
    outs = []
    for side, (x_in, mod) in enumerate(((x_lat, mod_l), (x_ctx, mod_c))):
        rg_x, rg_gate, ml_x, ml_z = streams[side]
        outs.append(even_out(x_in, mod, ln, (rg_sum[side], rg_gate, ml_sum[side], preps[side][4], ml_z),
                             vec(p['ml_norm_g']), vec(p['ml_skip']), w_out[:width], w_out[width:], alpha))
    return outs


def kernel(x, c, ctx, c_ctx, w_ada, b_ada, ln_g, ln_b, ffn_w_gate, ffn_w_up, ffn_w_down, ev_w_in, ev_w_out, rg_conv_w, rg_conv_b, rg_w_a, rg_b_a, rg_w_x, rg_b_x, rg_lambda, ml_conv_w, ml_conv_b, ml_w_q, ml_w_k, ml_w_v, ml_w_gate, ml_b_gate, ml_norm_g, ml_skip, od_w_in, od_w_out, cv_conv_w, cv_conv_b, cv_ln_g, cv_ln_b, da_lambda, da_subln_g):
    depth = w_ada.shape[0]
    d = x.shape[-1]
    alpha = (2 * depth) ** 0.25
    bsz, t_len = x.shape[:2]
    ctx_len = ctx.shape[1]
    rope = _rope_tables(t_len)
    x_lat, x_ctx = x, ctx
    cond_lat = jax.nn.silu(c)
    cond_ctx = jax.nn.silu(c_ctx)
    cw = cv_conv_w.shape[-1]
    dw = da_subln_g.shape[-1] * DA_HEADS
    for layer in range(depth):
        need_ctx = layer < depth - 1
        mod_l = (cond_lat @ w_ada[layer] + b_ada[layer]).reshape(bsz, N_SUB, 3, d)
        mod_c = (cond_ctx @ w_ada[layer] + b_ada[layer]).reshape(1, N_SUB, 3, d)
        ln = jnp.stack([ln_g[layer], ln_b[layer]], axis=1)
        ffn1 = _prep_ffn(ffn_w_gate[layer, 0], ffn_w_up[layer, 0], ffn_w_down[layer, 0])
        ffn2 = _prep_ffn(ffn_w_gate[layer, 1], ffn_w_up[layer, 1], ffn_w_down[layer, 1])
        x_lat = ffn_sublayer(x_lat, mod_l[:, 0], *ffn1, ln[0], alpha)
        x_ctx = ffn_sublayer(x_ctx, mod_c[:, 0], *ffn1, ln[0], alpha)
        i = layer // 2
        if layer % 2 == 0:
            p = dict(w_in=ev_w_in[i], w_out=ev_w_out[i], rg_conv_w=rg_conv_w[i], rg_conv_b=rg_conv_b[i],
                     rg_w_a=rg_w_a[i], rg_b_a=rg_b_a[i], rg_w_x=rg_w_x[i], rg_b_x=rg_b_x[i], rg_lambda=rg_lambda[i],
                     ml_conv_w=ml_conv_w[i], ml_conv_b=ml_conv_b[i], ml_w_q=ml_w_q[i], ml_w_k=ml_w_k[i],
                     ml_w_v=ml_w_v[i], ml_w_gate=ml_w_gate[i], ml_b_gate=ml_b_gate[i], ml_norm_g=ml_norm_g[i],
                     ml_skip=ml_skip[i])
            x_lat, x_ctx_new = _even_mixer(x_lat, x_ctx, mod_l[:, 1], mod_c[:, 1], ln[1], alpha, p)
        else:
            w_in = od_w_in[i].astype(BF16)
            w_vt = w_in[:, 2 * cw + 2 * dw:].T
            lam_init = 0.8 - 0.6 * math.exp(-0.3 * layer)
            lamf = da_lambda[i].astype(F32)
            lam = (jnp.exp(jnp.sum(lamf[0] * lamf[1])) - jnp.exp(jnp.sum(lamf[2] * lamf[3])) + lam_init).reshape(1)
            subln = da_subln_g[i].reshape(-1, 1)
            glu_l, q_l, k_l, vt_l = inproj_odd(x_lat, mod_l[:, 1], w_in, w_vt, rope, cw, dw)
            glu_c, q_c, k_c, vt_c = inproj_odd(x_ctx, mod_c[:, 1], w_in, w_vt, None, cw, dw)
            attn_l = diff_attention(q_l, k_c, vt_c, k_l, vt_l, lam, subln, 1.0 - lam_init)
            conv = (cv_conv_w[i], cv_conv_b[i], cv_ln_g[i], cv_ln_b[i])
            w_out = od_w_out[i].astype(BF16)
            w_parts = (w_out[:cw], w_out[cw:])
            x_lat = mixer_out(x_lat, mod_l[:, 1], ln[1], (conformer_conv(glu_l, *conv), attn_l), w_parts, alpha)
            if need_ctx:
                attn_c = diff_attention(q_c, k_c, vt_c, None, None, lam, subln, 1.0 - lam_init)
                x_ctx_new = mixer_out(x_ctx, mod_c[:, 1], ln[1], (conformer_conv(glu_c, *conv), attn_c), w_parts,
                                      alpha)
        x_lat = ffn_sublayer(x_lat, mod_l[:, 2], *ffn2, ln[2], alpha)
        if need_ctx:
            x_ctx = ffn_sublayer(x_ctx_new, mod_c[:, 2], *ffn2, ln[2], alpha)
    return x_lat
```
